```python
import jax, jax.numpy as jnp
from jax import lax
import numpy as np

D_MODEL = 1024
BATCH = 4
SEQ = 8192
DEPTH = 2

N_META = 16
N_MIXERS = 2
D_FF = 2816
CONV_WIDTH = 3
GLA_HEADS = 4
GLA_DK = D_MODEL // 2
GLA_DV = D_MODEL
GLA_HEAD_K = GLA_DK // GLA_HEADS
GLA_HEAD_V = GLA_DV // GLA_HEADS
GLA_GATE_RANK = 16
GLA_GATE_NORMALIZER = 16.0
GLA_CHUNK = 64
DEEPNORM_ALPHA = (2.0 * DEPTH) ** 0.25
DEEPNORM_BETA = (8.0 * DEPTH) ** -0.25
LN_EPS = 1e-5
RMS_EPS = 1e-6
N_CONV_LAYERS = (DEPTH + 1) // 2
N_GLA_LAYERS = DEPTH // 2

kernel_name = "hybrid_shortconv_gla_macaron_deepnorm"


def layer_norm(x, gain, bias):
    xf = x.astype(jnp.float32)
    mu = jnp.mean(xf, axis=-1, keepdims=True)
    var = jnp.mean(jnp.square(xf - mu), axis=-1, keepdims=True)
    y = (xf - mu) * lax.rsqrt(var + LN_EPS) * gain.astype(jnp.float32) + bias.astype(jnp.float32)
    return y.astype(x.dtype)


def swiglu_ffn(h, w_in, w_out):
    gate, up = jnp.split(h @ w_in, 2, axis=-1)
    return (jax.nn.silu(gate) * up) @ w_out


def short_conv_mixer(h, w_in, w_conv, w_out):
    bgate, cgate, u = jnp.split(h @ w_in, 3, axis=-1)
    u = cgate * u
    L = u.shape[1]
    up = jnp.pad(u, ((0, 0), (CONV_WIDTH - 1, 0), (0, 0)))
    conv = w_conv[0] * up[:, 0:L]
    for tap in range(1, CONV_WIDTH):
        conv = conv + w_conv[tap] * up[:, tap:tap + L]
    return (bgate * conv) @ w_out


def gla_mixer(h, w_in, w_gate_up, b_gate, norm_w, w_out):
    bsz, L, _ = h.shape
    f32 = jnp.float32
    splits = [GLA_DK, 2 * GLA_DK, 2 * GLA_DK + GLA_DV, 2 * GLA_DK + 2 * GLA_DV]
    q, k, v, g, gz = jnp.split(h @ w_in, splits, axis=-1)
    log_a = jax.nn.log_sigmoid((gz @ w_gate_up + b_gate).astype(f32)) / GLA_GATE_NORMALIZER
    q = q.astype(f32) * (GLA_HEAD_K ** -0.5)
    k = k.astype(f32)
    v = v.astype(f32)
    pad = (-L) % GLA_CHUNK
    padw = ((0, 0), (pad, 0), (0, 0))
    q, k, v, log_a = (jnp.pad(t, padw) for t in (q, k, v, log_a))
    Lp = L + pad
    nc = Lp // GLA_CHUNK

    def to_chunks(t, d):
        return t.reshape(bsz, nc, GLA_CHUNK, GLA_HEADS, d).transpose(1, 0, 3, 2, 4)

    q = to_chunks(q, GLA_HEAD_K)
    k = to_chunks(k, GLA_HEAD_K)
    v = to_chunks(v, GLA_HEAD_V)
    b = jnp.cumsum(to_chunks(log_a, GLA_HEAD_K), axis=3)
    b_ref = b[:, :, :, GLA_CHUNK // 2:GLA_CHUNK // 2 + 1, :]
    b_last = b[:, :, :, -1:, :]
    scores = jnp.einsum('nbhik,nbhjk->nbhij', q * jnp.exp(b - b_ref), k * jnp.exp(b_ref - b))
    causal = jnp.tril(jnp.ones((GLA_CHUNK, GLA_CHUNK), dtype=bool))
    scores = jnp.where(causal, scores, 0.0)
    o_intra = jnp.einsum('nbhij,nbhjv->nbhiv', scores, v)
    q_inter = q * jnp.exp(b)
    k_state = k * jnp.exp(b_last - b)
    decay = jnp.exp(b_last[:, :, :, 0, :])

    def step(state, xs):
        qi, ks, vi, dec = xs
        o = jnp.einsum('bhck,bhkv->bhcv', qi, state)
        state = dec[..., None] * state + jnp.einsum('bhck,bhcv->bhkv', ks, vi)
        return state, o

    s0 = jnp.zeros((bsz, GLA_HEADS, GLA_HEAD_K, GLA_HEAD_V), f32)
    _, o_inter = lax.scan(step, s0, (q_inter, k_state, v, decay))
    o = (o_intra + o_inter).transpose(1, 0, 3, 2, 4).reshape(bsz, Lp, GLA_HEADS, GLA_HEAD_V)[:, pad:]
    o = o * lax.rsqrt(jnp.mean(jnp.square(o), axis=-1, keepdims=True) + RMS_EPS) * norm_w.astype(f32)
    o = o.reshape(bsz, L, GLA_DV).astype(h.dtype) * jax.nn.silu(g)
    return o @ w_out


def setup_inputs(seed: int = 0) -> dict:
    key = jax.random.key(seed)
    ks = jax.random.split(key, 16)
    nrm = jax.random.normal
    D, F = D_MODEL, D_FF
    gla_in_width = 2 * GLA_DK + 2 * GLA_DV + GLA_GATE_RANK
    return {
        "x": nrm(ks[0], (BATCH, SEQ, D), jnp.float32),
        "meta_tokens": nrm(ks[1], (N_META, D), jnp.float32),
        "ln_gain": 1.0 + 0.02 * nrm(ks[2], (DEPTH, 3, D), jnp.float32),
        "ln_bias": 0.02 * nrm(ks[3], (DEPTH, 3, D), jnp.float32),
        "ffn_w_in": nrm(ks[4], (DEPTH, 2, D, 2 * F), jnp.float32) * D ** -0.5,
        "ffn_w_out": nrm(ks[5], (DEPTH, 2, F, D), jnp.float32) * (F ** -0.5 * DEEPNORM_BETA),
        "conv_w_in": nrm(ks[6], (N_CONV_LAYERS, D, 3 * D), jnp.float32) * D ** -0.5,
        "conv_w": nrm(ks[7], (N_CONV_LAYERS, CONV_WIDTH, D), jnp.float32) * CONV_WIDTH ** -0.5,
        "conv_w_out": nrm(ks[8], (N_CONV_LAYERS, D, D), jnp.float32) * (D ** -0.5 * DEEPNORM_BETA),
        "gla_w_in": nrm(ks[9], (N_GLA_LAYERS, D, gla_in_width), jnp.float32) * D ** -0.5,
        "gla_w_gate_up": nrm(ks[10], (N_GLA_LAYERS, GLA_GATE_RANK, GLA_DK), jnp.float32) * GLA_GATE_RANK ** -0.5,
        "gla_b_gate": 0.01 * nrm(ks[11], (N_GLA_LAYERS, GLA_DK), jnp.float32),
        "gla_norm_w": 1.0 + 0.02 * nrm(ks[12], (N_GLA_LAYERS, GLA_HEAD_V), jnp.float32),
        "gla_w_out": nrm(ks[13], (N_GLA_LAYERS, GLA_DV, D), jnp.float32) * (GLA_DV ** -0.5 * DEEPNORM_BETA),
    }


def reference(x, meta_tokens, ln_gain, ln_bias, ffn_w_in, ffn_w_out, conv_w_in, conv_w, conv_w_out,
              gla_w_in, gla_w_gate_up, gla_b_gate, gla_norm_w, gla_w_out):
    bsz = x.shape[0]
    meta = jnp.broadcast_to(meta_tokens[None].astype(x.dtype), (bsz, N_META, D_MODEL))
    h = jnp.concatenate([meta, x], axis=1)
    for i in range(DEPTH):
        h = layer_norm(DEEPNORM_ALPHA * h + 0.5 * swiglu_ffn(h, ffn_w_in[i, 0], ffn_w_out[i, 0]),
                       ln_gain[i, 0], ln_bias[i, 0])
        j = i // N_MIXERS
        if i % N_MIXERS == 0:
            mix = short_conv_mixer(h, conv_w_in[j], conv_w[j], conv_w_out[j])
        else:
            mix = gla_mixer(h, gla_w_in[j], gla_w_gate_up[j], gla_b_gate[j], gla_norm_w[j], gla_w_out[j])
        h = layer_norm(DEEPNORM_ALPHA * h + mix, ln_gain[i, 1], ln_bias[i, 1])
        h = layer_norm(DEEPNORM_ALPHA * h + 0.5 * swiglu_ffn(h, ffn_w_in[i, 1], ffn_w_out[i, 1]),
                       ln_gain[i, 2], ln_bias[i, 2])
    return h[:, N_META:]
```

```python
import functools

import jax
import jax.numpy as jnp
from jax import lax
from jax.experimental import pallas as pl
from jax.experimental.pallas import tpu as pltpu

D_MODEL = 1024
DEPTH = 2
N_META = 16
D_FF = 2816
CONV_WIDTH = 3
GLA_HEADS = 4
GLA_DK = D_MODEL // 2
GLA_DV = D_MODEL
GLA_HEAD_K = GLA_DK // GLA_HEADS
GLA_HEAD_V = GLA_DV // GLA_HEADS
GLA_GATE_RANK = 16
GLA_GATE_NORMALIZER = 16.0
DEEPNORM_ALPHA = (2.0 * DEPTH) ** 0.25
LN_EPS = 1e-5
RMS_EPS = 1e-6

GLA_CHUNK = 48
CONV_CARRY_ROWS = 8

FFN_TILE = 576
SEQ_TILE = 432
VMEM_LIMIT_BYTES = 56 * 1024 * 1024

F32 = jnp.float32
BF16 = jnp.bfloat16


def _dot(a, b):
    return jnp.dot(a, b, preferred_element_type=F32)


def _layer_norm(y, gain, bias):
    mu = jnp.mean(y, axis=-1, keepdims=True)
    yc = y - mu
    var = jnp.mean(yc * yc, axis=-1, keepdims=True)
    return yc * lax.rsqrt(var + LN_EPS) * gain + bias


def _resident(shape):
    zeros = (0,) * len(shape)
    return pl.BlockSpec(shape, lambda *_: zeros, pipeline_mode=pl.Buffered(1))


def _ffn_kernel(h_ref, w_gate_ref, w_up_ref, w_out_ref, gain_ref, bias_ref, o_ref):
    x = h_ref[...]
    xb = x.astype(BF16)
    gate = _dot(xb, w_gate_ref[...])
    up = _dot(xb, w_up_ref[...])
    act = (gate * jax.nn.sigmoid(gate) * up).astype(BF16)
    y = DEEPNORM_ALPHA * x + 0.5 * _dot(act, w_out_ref[...])
    o_ref[...] = _layer_norm(y, gain_ref[...], bias_ref[...])


def _ffn_block(h, w_gate, w_up, w_out, gain, bias):
    n, d = h.shape
    f = w_gate.shape[1]
    tm = FFN_TILE
    assert n % tm == 0
    return pl.pallas_call(
        _ffn_kernel,
        grid=(n // tm,),
        in_specs=[
            pl.BlockSpec((tm, d), lambda i: (i, 0)),
            _resident((d, f)), _resident((d, f)), _resident((f, d)),
            _resident((1, d)), _resident((1, d)),
        ],
        out_specs=pl.BlockSpec((tm, d), lambda i: (i, 0)),
        out_shape=jax.ShapeDtypeStruct((n, d), F32),
        compiler_params=pltpu.CompilerParams(
            dimension_semantics=("arbitrary",), vmem_limit_bytes=VMEM_LIMIT_BYTES),
        name="ffn_block",
    )(h, w_gate, w_up, w_out, gain, bias)


def _conv_kernel(h_ref, w_b_ref, w_c_ref, w_u_ref, w_conv_ref, w_out_ref, gain_ref, bias_ref,
                 o_ref, carry_ref):
    @pl.when(pl.program_id(1) == 0)
    def _():
        carry_ref[...] = jnp.zeros_like(carry_ref)

    x = h_ref[0]
    tm = x.shape[0]
    xb = x.astype(BF16)
    bgate = _dot(xb, w_b_ref[...])
    u = _dot(xb, w_c_ref[...]) * _dot(xb, w_u_ref[...])
    prev = carry_ref[...]
    row = lax.broadcasted_iota(jnp.int32, (tm, 1), 0)
    last = prev[CONV_CARRY_ROWS - 1:CONV_CARRY_ROWS]
    u1 = jnp.where(row == 0, last, pltpu.roll(u, 1, 0))
    u2 = jnp.where(row == 0, prev[CONV_CARRY_ROWS - 2:CONV_CARRY_ROWS - 1],
                   jnp.where(row == 1, last, pltpu.roll(u, 2, 0)))
    carry_ref[...] = u[tm - CONV_CARRY_ROWS:]
    w = w_conv_ref[...]
    conv = w[0:1] * u2 + w[1:2] * u1 + w[2:3] * u
    mix = _dot((bgate * conv).astype(BF16), w_out_ref[...])
    o_ref[0] = _layer_norm(DEEPNORM_ALPHA * x + mix, gain_ref[...], bias_ref[...])


def _conv_block(h, w_b, w_c, w_u, w_conv, w_out, gain, bias):
    bsz, length, d = h.shape
    tm = SEQ_TILE
    assert length % tm == 0 and CONV_WIDTH - 1 <= CONV_CARRY_ROWS <= tm
    return pl.pallas_call(
        _conv_kernel,
        grid=(bsz, length // tm),
        in_specs=[
            pl.BlockSpec((1, tm, d), lambda b, t: (b, t, 0)),
            _resident((d, d)), _resident((d, d)), _resident((d, d)),
            _resident((CONV_WIDTH, d)), _resident((d, d)),
            _resident((1, d)), _resident((1, d)),
        ],
        out_specs=pl.BlockSpec((1, tm, d), lambda b, t: (b, t, 0)),
        out_shape=jax.ShapeDtypeStruct((bsz, length, d), F32),
        scratch_shapes=[pltpu.VMEM((CONV_CARRY_ROWS, d), F32)],
        compiler_params=pltpu.CompilerParams(
            dimension_semantics=("arbitrary", "arbitrary"), vmem_limit_bytes=VMEM_LIMIT_BYTES),
        name="conv_block",
    )(h, w_b, w_c, w_u, w_conv, w_out, gain, bias)


def _split3_bf16(a):
    hi = a.astype(BF16)
    r = a - hi.astype(F32)
    mid = r.astype(BF16)
    lo = (r - mid.astype(F32)).astype(BF16)
    return hi, mid, lo


def _gla_kernel(h_ref, w_q_ref, w_k_ref, w_v_ref, w_g_ref, w_gz_ref, w_gu_ref, b_gate_ref,
                norm_w_ref, w_out_ref, gain_ref, bias_ref, o_ref,
                state_ref, q_ref, k_ref, v_ref, la_ref, att_ref):
    @pl.when(pl.program_id(1) == 0)
    def _():
        state_ref[...] = jnp.zeros_like(state_ref)

    x = h_ref[0]
    tm = x.shape[0]
    xb = x.astype(BF16)
    q_ref[...] = _dot(xb, w_q_ref[...]) * (GLA_HEAD_K ** -0.5)
    k_ref[...] = _dot(xb, w_k_ref[...])
    v_ref[...] = _dot(xb, w_v_ref[...]).astype(BF16)
    gz = _dot(xb, w_gz_ref[...])
    z = _dot(gz.astype(BF16), w_gu_ref[...]) + b_gate_ref[...]
    la_ref[...] = (jnp.minimum(z, 0.0) - jnp.log1p(jnp.exp(-jnp.abs(z)))) * (1.0 / GLA_GATE_NORMALIZER)

    c = GLA_CHUNK
    ri = lax.broadcasted_iota(jnp.int32, (c, c), 0)
    ci = lax.broadcasted_iota(jnp.int32, (c, c), 1)
    causal = ci <= ri
    tril = causal.astype(BF16)

    def chunk_body(ic, carry):
        r0 = pl.multiple_of(ic * c, c)
        la = la_ref[pl.ds(r0, c), :]
        hi, mid, lo = _split3_bf16(la)
        b = _dot(tril, hi) + _dot(tril, mid) + _dot(tril, lo)
        b_mid = b[c // 2:c // 2 + 1]
        b_last = b[c - 1:c]
        q_t = q_ref[pl.ds(r0, c), :] * jnp.exp(b - b_mid)
        k_t = k_ref[pl.ds(r0, c), :] * jnp.exp(b_mid - b)
        q_in = (q_t * jnp.exp(b_mid)).astype(BF16)
        k_st = (k_t * jnp.exp(b_last - b_mid)).astype(BF16)
        q_t = q_t.astype(BF16)
        k_t = k_t.astype(BF16)
        decay = jnp.exp(b_last)
        v_c = v_ref[pl.ds(r0, c), :]
        for hd in range(GLA_HEADS):
            ks = slice(hd * GLA_HEAD_K, (hd + 1) * GLA_HEAD_K)
            vs = slice(hd * GLA_HEAD_V, (hd + 1) * GLA_HEAD_V)
            s_prev = state_ref[hd]
            scores = lax.dot_general(q_t[:, ks], k_t[:, ks], (((1,), (1,)), ((), ())),
                                     preferred_element_type=F32)
            scores = jnp.where(causal, scores, 0.0).astype(BF16)
            att_ref[pl.ds(r0, c), vs] = _dot(scores, v_c[:, vs]) + _dot(q_in[:, ks], s_prev.astype(BF16))
            upd = lax.dot_general(k_st[:, ks], v_c[:, vs], (((0,), (0,)), ((), ())),
                                  preferred_element_type=F32)
            dec_col = jnp.transpose(jnp.broadcast_to(decay[:, ks], (GLA_HEAD_K, GLA_HEAD_K)))
            dec_col = jnp.concatenate([dec_col] * (GLA_HEAD_V // GLA_HEAD_K), axis=1)
            state_ref[hd] = dec_col * s_prev + upd
        return carry

    lax.fori_loop(0, tm // c, chunk_body, 0)

    gate = _dot(xb, w_g_ref[...])
    gate = gate * jax.nn.sigmoid(gate)
    norm_w = norm_w_ref[...]
    heads = []
    for hd in range(GLA_HEADS):
        vs = slice(hd * GLA_HEAD_V, (hd + 1) * GLA_HEAD_V)
        o = att_ref[:, vs]
        o = o * lax.rsqrt(jnp.mean(o * o, axis=-1, keepdims=True) + RMS_EPS) * norm_w
        heads.append((o * gate[:, vs]).astype(BF16))
    mix = _dot(jnp.concatenate(heads, axis=1), w_out_ref[...])
    o_ref[0] = _layer_norm(DEEPNORM_ALPHA * x + mix, gain_ref[...], bias_ref[...])


def _gla_block(h, w_q, w_k, w_v, w_g, w_gz, w_gu, b_gate, norm_w, w_out, gain, bias):
    bsz, length, d = h.shape
    tm = SEQ_TILE
    assert length % tm == 0 and tm % GLA_CHUNK == 0
    return pl.pallas_call(
        _gla_kernel,
        grid=(bsz, length // tm),
        in_specs=[
            pl.BlockSpec((1, tm, d), lambda b, t: (b, t, 0)),
            _resident((d, GLA_DK)), _resident((d, GLA_DK)), _resident((d, GLA_DV)), _resident((d, GLA_DV)),
            _resident((d, GLA_GATE_RANK)), _resident((GLA_GATE_RANK, GLA_DK)), _resident((1, GLA_DK)),
            _resident((1, GLA_HEAD_V)), _resident((GLA_DV, d)),
            _resident((1, d)), _resident((1, d)),
        ],
        out_specs=pl.BlockSpec((1, tm, d), lambda b, t: (b, t, 0)),
        out_shape=jax.ShapeDtypeStruct((bsz, length, d), F32),
        scratch_shapes=[
            pltpu.VMEM((GLA_HEADS, GLA_HEAD_K, GLA_HEAD_V), F32),
            pltpu.VMEM((tm, GLA_DK), F32),
            pltpu.VMEM((tm, GLA_DK), F32),
            pltpu.VMEM((tm, GLA_DV), BF16),
            pltpu.VMEM((tm, GLA_DK), F32),
            pltpu.VMEM((tm, GLA_DV), F32),
        ],
        compiler_params=pltpu.CompilerParams(
            dimension_semantics=("arbitrary", "arbitrary"), vmem_limit_bytes=VMEM_LIMIT_BYTES),
        name="gla_block",
    )(h, w_q, w_k, w_v, w_g, w_gz, w_gu, b_gate, norm_w, w_out, gain, bias)


def kernel(x, meta_tokens, ln_gain, ln_bias, ffn_w_in, ffn_w_out, conv_w_in, conv_w, conv_w_out,
           gla_w_in, gla_w_gate_up, gla_b_gate, gla_norm_w, gla_w_out):
    bsz, seq, d = x.shape
    length = N_META + seq
    meta = jnp.broadcast_to(meta_tokens[None].astype(x.dtype), (bsz, N_META, d))
    h = jnp.concatenate([meta, x], axis=1)

    def ffn(h, i, half):
        w_in = ffn_w_in[i, half].astype(BF16)
        out = _ffn_block(h.reshape(bsz * length, d), w_in[:, :D_FF], w_in[:, D_FF:],
                         ffn_w_out[i, half].astype(BF16),
                         ln_gain[i, 2 * half][None], ln_bias[i, 2 * half][None])
        return out.reshape(bsz, length, d)

    for i in range(DEPTH):
        h = ffn(h, i, 0)
        j = i // 2
        gain, bias = ln_gain[i, 1][None], ln_bias[i, 1][None]
        if i % 2 == 0:
            w_in = conv_w_in[j].astype(BF16)
            h = _conv_block(h, w_in[:, :d], w_in[:, d:2 * d], w_in[:, 2 * d:], conv_w[j],
                            conv_w_out[j].astype(BF16), gain, bias)
        else:
            w_in = gla_w_in[j].astype(BF16)
            o_q, o_k, o_v, o_g = GLA_DK, 2 * GLA_DK, 2 * GLA_DK + GLA_DV, 2 * GLA_DK + 2 * GLA_DV
            h = _gla_block(h, w_in[:, :o_q], w_in[:, o_q:o_k], w_in[:, o_k:o_v], w_in[:, o_v:o_g],
                           w_in[:, o_g:], gla_w_gate_up[j].astype(BF16), gla_b_gate[j][None],
                           gla_norm_w[j][None], gla_w_out[j].astype(BF16), gain, bias)
        h = ffn(h, i, 1)
    return h[:, N_META:]
```

```python
import functools

import jax
import jax.numpy as jnp
from jax import lax
from jax.experimental import pallas as pl
from jax.experimental.pallas import tpu as pltpu

D_MODEL = 1024
DEPTH = 2
N_META = 16
D_FF = 2816
CONV_WIDTH = 3
GLA_HEADS = 4
GLA_DK = D_MODEL // 2
GLA_DV = D_MODEL
GLA_HEAD_K = GLA_DK // GLA_HEADS
GLA_HEAD_V = GLA_DV // GLA_HEADS
GLA_GATE_RANK = 16
GLA_GATE_NORMALIZER = 16.0
DEEPNORM_ALPHA = (2.0 * DEPTH) ** 0.25
LN_EPS = 1e-5
RMS_EPS = 1e-6

GLA_CHUNK = 48
CONV_CARRY_ROWS = 8

FFN_TILE = 576
SEQ_TILE = 432
VMEM_LIMIT_BYTES = 56 * 1024 * 1024

F32 = jnp.float32
BF16 = jnp.bfloat16


def _dot(a, b):
    return jnp.dot(a, b, preferred_element_type=F32)


def _layer_norm(y, gain, bias):
    mu = jnp.mean(y, axis=-1, keepdims=True)
    yc = y - mu
    var = jnp.mean(yc * yc, axis=-1, keepdims=True)
    return yc * lax.rsqrt(var + LN_EPS) * gain + bias


def _resident(shape):
    zeros = (0,) * len(shape)
    return pl.BlockSpec(shape, lambda *_: zeros, pipeline_mode=pl.Buffered(1))


def _ffn_kernel(h_ref, w_gate_ref, w_up_ref, w_out_ref, gain_ref, bias_ref, o_ref):
    x = h_ref[...]
    xb = x.astype(BF16)
    gate = _dot(xb, w_gate_ref[...])
    up = _dot(xb, w_up_ref[...])
    act = (gate * jax.nn.sigmoid(gate) * up).astype(BF16)
    y = DEEPNORM_ALPHA * x + 0.5 * _dot(act, w_out_ref[...])
    o_ref[...] = _layer_norm(y, gain_ref[...], bias_ref[...])


def _ffn_block(h, w_gate, w_up, w_out, gain, bias):
    n, d = h.shape
    f = w_gate.shape[1]
    tm = FFN_TILE
    assert n % tm == 0
    return pl.pallas_call(
        _ffn_kernel,
        grid=(n // tm,),
        in_specs=[
            pl.BlockSpec((tm, d), lambda i: (i, 0)),
            _resident((d, f)), _resident((d, f)), _resident((f, d)),
            _resident((1, d)), _resident((1, d)),
        ],
        out_specs=pl.BlockSpec((tm, d), lambda i: (i, 0)),
        out_shape=jax.ShapeDtypeStruct((n, d), F32),
        compiler_params=pltpu.CompilerParams(
            dimension_semantics=("arbitrary",), vmem_limit_bytes=VMEM_LIMIT_BYTES),
        name="ffn_block",
    )(h, w_gate, w_up, w_out, gain, bias)


def _conv_kernel(h_ref, w_b_ref, w_c_ref, w_u_ref, w_conv_ref, w_out_ref, gain_ref, bias_ref,
                 o_ref, carry_ref):
    @pl.when(pl.program_id(1) == 0)
    def _():
        carry_ref[...] = jnp.zeros_like(carry_ref)

    x = h_ref[0]
    tm = x.shape[0]
    xb = x.astype(BF16)
    bgate = _dot(xb, w_b_ref[...])
    u = _dot(xb, w_c_ref[...]) * _dot(xb, w_u_ref[...])
    prev = carry_ref[...]
    row = lax.broadcasted_iota(jnp.int32, (tm, 1), 0)
    last = prev[CONV_CARRY_ROWS - 1:CONV_CARRY_ROWS]
    u1 = jnp.where(row == 0, last, pltpu.roll(u, 1, 0))
    u2 = jnp.where(row == 0, prev[CONV_CARRY_ROWS - 2:CONV_CARRY_ROWS - 1],
                   jnp.where(row == 1, last, pltpu.roll(u, 2, 0)))
    carry_ref[...] = u[tm - CONV_CARRY_ROWS:]
    w = w_conv_ref[...]
    conv = w[0:1] * u2 + w[1:2] * u1 + w[2:3] * u
    mix = _dot((bgate * conv).astype(BF16), w_out_ref[...])
    o_ref[0] = _layer_norm(DEEPNORM_ALPHA * x + mix, gain_ref[...], bias_ref[...])


def _conv_block(h, w_b, w_c, w_u, w_conv, w_out, gain, bias):
    bsz, length, d = h.shape
    tm = SEQ_TILE
    assert length % tm == 0 and CONV_WIDTH - 1 <= CONV_CARRY_ROWS <= tm
    return pl.pallas_call(
        _conv_kernel,
        grid=(bsz, length // tm),
        in_specs=[
            pl.BlockSpec((1, tm, d), lambda b, t: (b, t, 0)),
            _resident((d, d)), _resident((d, d)), _resident((d, d)),
            _resident((CONV_WIDTH, d)), _resident((d, d)),
            _resident((1, d)), _resident((1, d)),
        ],
        out_specs=pl.BlockSpec((1, tm, d), lambda b, t: (b, t, 0)),
        out_shape=jax.ShapeDtypeStruct((bsz, length, d), F32),
        scratch_shapes=[pltpu.VMEM((CONV_CARRY_ROWS, d), F32)],
        compiler_params=pltpu.CompilerParams(
            dimension_semantics=("arbitrary", "arbitrary"), vmem_limit_bytes=VMEM_LIMIT_BYTES),
        name="conv_block",
    )(h, w_b, w_c, w_u, w_conv, w_out, gain, bias)


def _split3_bf16(a):
    hi = a.astype(BF16)
    r = a - hi.astype(F32)
    mid = r.astype(BF16)
    lo = (r - mid.astype(F32)).astype(BF16)
    return hi, mid, lo


def _gla_kernel(h_ref, w_q_ref, w_k_ref, w_v_ref, w_g_ref, w_gz_ref, w_gu_ref, b_gate_ref,
                norm_w_ref, w_out_ref, gain_ref, bias_ref, o_ref,
                state_ref, q_ref, k_ref, v_ref, la_ref, att_ref, qin_ref, p_ref, upd_ref,
                b_ref, qt_ref, kt_ref, kst_ref, kstT_ref):
    @pl.when(pl.program_id(1) == 0)
    def _():
        state_ref[...] = jnp.zeros_like(state_ref)

    x = h_ref[0]
    tm = x.shape[0]
    xb = x.astype(BF16)
    q_ref[...] = _dot(xb, w_q_ref[...]) * (GLA_HEAD_K ** -0.5)
    k_ref[...] = _dot(xb, w_k_ref[...])
    v_ref[...] = _dot(xb, w_v_ref[...]).astype(BF16)
    gz = _dot(xb, w_gz_ref[...])
    z = _dot(gz.astype(BF16), w_gu_ref[...]) + b_gate_ref[...]
    la_ref[...] = (jnp.minimum(z, 0.0) - jnp.log1p(jnp.exp(-jnp.abs(z)))) * (1.0 / GLA_GATE_NORMALIZER)

    c = GLA_CHUNK
    n_chunks = tm // c
    ri = lax.broadcasted_iota(jnp.int32, (c, c), 0)
    ci = lax.broadcasted_iota(jnp.int32, (c, c), 1)
    causal = ci <= ri
    tril = causal.astype(BF16)
    head_k = [slice(hd * GLA_HEAD_K, (hd + 1) * GLA_HEAD_K) for hd in range(GLA_HEADS)]
    head_v = [slice(hd * GLA_HEAD_V, (hd + 1) * GLA_HEAD_V) for hd in range(GLA_HEADS)]

    chunk_rows = [slice(ic * c, (ic + 1) * c) for ic in range(n_chunks)]
    for rows in chunk_rows:
        hi, mid, lo = _split3_bf16(la_ref[rows, :])
        b_ref[rows, :] = _dot(tril, hi) + _dot(tril, mid) + _dot(tril, lo)
    decays = []
    for rows in chunk_rows:
        b = b_ref[rows, :]
        b_mid = b[c // 2:c // 2 + 1]
        b_last = b[c - 1:c]
        q_t = q_ref[rows, :] * jnp.exp(b - b_mid)
        k_t = k_ref[rows, :] * jnp.exp(b_mid - b)
        qin_ref[rows, :] = (q_t * jnp.exp(b_mid)).astype(BF16)
        kst_ref[rows, :] = (k_t * jnp.exp(b_last - b_mid)).astype(BF16)
        qt_ref[rows, :] = q_t.astype(BF16)
        kt_ref[rows, :] = k_t.astype(BF16)
        decays.append(jnp.exp(b_last))
    for ic, rows in enumerate(chunk_rows):
        kstT_ref[ic] = jnp.transpose(kst_ref[rows, :])
    for ic, rows in enumerate(chunk_rows):
        for hd in range(GLA_HEADS):
            scores = lax.dot_general(qt_ref[rows, head_k[hd]], kt_ref[rows, head_k[hd]],
                                     (((1,), (1,)), ((), ())), preferred_element_type=F32)
            p_ref[hd, rows, :] = jnp.where(causal, scores, 0.0).astype(BF16)
            upd_ref[ic, hd] = _dot(kstT_ref[ic, head_k[hd], :], v_ref[rows, head_v[hd]])

    pad = [jnp.zeros_like(decays[0])] * (-n_chunks % 8)
    dec_rows = jnp.concatenate(decays + pad, axis=0)
    dec_cols = [jnp.transpose(dec_rows[:, head_k[hd]]) for hd in range(GLA_HEADS)]

    for ic in range(n_chunks):
        rows = slice(ic * c, (ic + 1) * c)
        for hd in range(GLA_HEADS):
            s_prev = state_ref[hd]
            att_ref[rows, head_v[hd]] = (_dot(p_ref[hd, rows, :], v_ref[rows, head_v[hd]])
                                         + _dot(qin_ref[rows, head_k[hd]], s_prev.astype(BF16)))
            state_ref[hd] = dec_cols[hd][:, ic:ic + 1] * s_prev + upd_ref[ic, hd]

    gate = _dot(xb, w_g_ref[...])
    gate = gate * jax.nn.sigmoid(gate)
    norm_w = norm_w_ref[...]
    heads = []
    for hd in range(GLA_HEADS):
        o = att_ref[:, head_v[hd]]
        o = o * lax.rsqrt(jnp.mean(o * o, axis=-1, keepdims=True) + RMS_EPS) * norm_w
        heads.append((o * gate[:, head_v[hd]]).astype(BF16))
    mix = _dot(jnp.concatenate(heads, axis=1), w_out_ref[...])
    o_ref[0] = _layer_norm(DEEPNORM_ALPHA * x + mix, gain_ref[...], bias_ref[...])


def _gla_block(h, w_q, w_k, w_v, w_g, w_gz, w_gu, b_gate, norm_w, w_out, gain, bias):
    bsz, length, d = h.shape
    tm = SEQ_TILE
    assert length % tm == 0 and tm % GLA_CHUNK == 0
    return pl.pallas_call(
        _gla_kernel,
        grid=(bsz, length // tm),
        in_specs=[
            pl.BlockSpec((1, tm, d), lambda b, t: (b, t, 0)),
            _resident((d, GLA_DK)), _resident((d, GLA_DK)), _resident((d, GLA_DV)), _resident((d, GLA_DV)),
            _resident((d, GLA_GATE_RANK)), _resident((GLA_GATE_RANK, GLA_DK)), _resident((1, GLA_DK)),
            _resident((1, GLA_HEAD_V)), _resident((GLA_DV, d)),
            _resident((1, d)), _resident((1, d)),
        ],
        out_specs=pl.BlockSpec((1, tm, d), lambda b, t: (b, t, 0)),
        out_shape=jax.ShapeDtypeStruct((bsz, length, d), F32),
        scratch_shapes=[
            pltpu.VMEM((GLA_HEADS, GLA_HEAD_K, GLA_HEAD_V), F32),
            pltpu.VMEM((tm, GLA_DK), F32),
            pltpu.VMEM((tm, GLA_DK), F32),
            pltpu.VMEM((tm, GLA_DV), BF16),
            pltpu.VMEM((tm, GLA_DK), F32),
            pltpu.VMEM((tm, GLA_DV), F32),
            pltpu.VMEM((tm, GLA_DK), BF16),
            pltpu.VMEM((GLA_HEADS, tm, GLA_CHUNK), BF16),
            pltpu.VMEM((tm // GLA_CHUNK, GLA_HEADS, GLA_HEAD_K, GLA_HEAD_V), F32),
            pltpu.VMEM((tm, GLA_DK), F32),
            pltpu.VMEM((tm, GLA_DK), BF16),
            pltpu.VMEM((tm, GLA_DK), BF16),
            pltpu.VMEM((tm, GLA_DK), BF16),
            pltpu.VMEM((tm // GLA_CHUNK, GLA_DK, GLA_CHUNK), BF16),
        ],
        compiler_params=pltpu.CompilerParams(
            dimension_semantics=("arbitrary", "arbitrary"), vmem_limit_bytes=VMEM_LIMIT_BYTES),
        name="gla_block",
    )(h, w_q, w_k, w_v, w_g, w_gz, w_gu, b_gate, norm_w, w_out, gain, bias)


def kernel(x, meta_tokens, ln_gain, ln_bias, ffn_w_in, ffn_w_out, conv_w_in, conv_w, conv_w_out,
           gla_w_in, gla_w_gate_up, gla_b_gate, gla_norm_w, gla_w_out):
    bsz, seq, d = x.shape
    length = N_META + seq
    meta = jnp.broadcast_to(meta_tokens[None].astype(x.dtype), (bsz, N_META, d))
    h = jnp.concatenate([meta, x], axis=1)

    def ffn(h, i, half):
        w_in = ffn_w_in[i, half].astype(BF16)
        out = _ffn_block(h.reshape(bsz * length, d), w_in[:, :D_FF], w_in[:, D_FF:],
                         ffn_w_out[i, half].astype(BF16),
                         ln_gain[i, 2 * half][None], ln_bias[i, 2 * half][None])
        return out.reshape(bsz, length, d)

    for i in range(DEPTH):
        h = ffn(h, i, 0)
        j = i // 2
        gain, bias = ln_gain[i, 1][None], ln_bias[i, 1][None]
        if i % 2 == 0:
            w_in = conv_w_in[j].astype(BF16)
            h = _conv_block(h, w_in[:, :d], w_in[:, d:2 * d], w_in[:, 2 * d:], conv_w[j],
                            conv_w_out[j].astype(BF16), gain, bias)
        else:
            w_in = gla_w_in[j].astype(BF16)
            o_q, o_k, o_v, o_g = GLA_DK, 2 * GLA_DK, 2 * GLA_DK + GLA_DV, 2 * GLA_DK + 2 * GLA_DV
            h = _gla_block(h, w_in[:, :o_q], w_in[:, o_q:o_k], w_in[:, o_k:o_v], w_in[:, o_v:o_g],
                           w_in[:, o_g:], gla_w_gate_up[j].astype(BF16), gla_b_gate[j][None],
                           gla_norm_w[j][None], gla_w_out[j].astype(BF16), gain, bias)
        h = ffn(h, i, 1)
    return h[:, N_META:]
```

```python
import functools

import jax
import jax.numpy as jnp
from jax import lax
from jax.experimental import pallas as pl
from jax.experimental.pallas import tpu as pltpu

D_MODEL = 1024
DEPTH = 2
N_META = 16
D_FF = 2816
CONV_WIDTH = 3
GLA_HEADS = 4
GLA_DK = D_MODEL // 2
GLA_DV = D_MODEL
GLA_HEAD_K = GLA_DK // GLA_HEADS
GLA_HEAD_V = GLA_DV // GLA_HEADS
GLA_GATE_RANK = 16
GLA_GATE_NORMALIZER = 16.0
DEEPNORM_ALPHA = (2.0 * DEPTH) ** 0.25
LN_EPS = 1e-5
RMS_EPS = 1e-6

GLA_CHUNK = 48
CONV_CARRY_ROWS = 8

FFN_TILE = 1216
FFN_SUBTILES = 2
CONV_TILE = 912
CONV_SUBTILES = 3
SEQ_TILE = 432
VMEM_LIMIT_BYTES = 56 * 1024 * 1024

F32 = jnp.float32
BF16 = jnp.bfloat16


def _dot(a, b):
    return jnp.dot(a, b, preferred_element_type=F32)


def _layer_norm(y, gain, bias):
    mu = jnp.mean(y, axis=-1, keepdims=True)
    yc = y - mu
    var = jnp.mean(yc * yc, axis=-1, keepdims=True)
    return yc * lax.rsqrt(var + LN_EPS) * gain + bias


def _resident(shape):
    zeros = (0,) * len(shape)
    return pl.BlockSpec(shape, lambda *_: zeros, pipeline_mode=pl.Buffered(1))


def _ffn_kernel(h_ref, w_gate_ref, w_up_ref, w_out_ref, gain_ref, bias_ref, o_ref):
    sub = h_ref.shape[0] // FFN_SUBTILES
    for s in range(FFN_SUBTILES):
        rows = slice(s * sub, (s + 1) * sub)
        x = h_ref[rows, :]
        xb = x.astype(BF16)
        gate = _dot(xb, w_gate_ref[...])
        up = _dot(xb, w_up_ref[...])
        act = (gate * jax.nn.sigmoid(gate) * up).astype(BF16)
        y = DEEPNORM_ALPHA * x + 0.5 * _dot(act, w_out_ref[...])
        o_ref[rows, :] = _layer_norm(y, gain_ref[...], bias_ref[...])


def _ffn_block(h, w_gate, w_up, w_out, gain, bias):
    n, d = h.shape
    f = w_gate.shape[1]
    tm = FFN_TILE
    assert n % tm == 0
    return pl.pallas_call(
        _ffn_kernel,
        grid=(n // tm,),
        in_specs=[
            pl.BlockSpec((tm, d), lambda i: (i, 0)),
            _resident((d, f)), _resident((d, f)), _resident((f, d)),
            _resident((1, d)), _resident((1, d)),
        ],
        out_specs=pl.BlockSpec((tm, d), lambda i: (i, 0)),
        out_shape=jax.ShapeDtypeStruct((n, d), F32),
        compiler_params=pltpu.CompilerParams(
            dimension_semantics=("arbitrary",), vmem_limit_bytes=VMEM_LIMIT_BYTES),
        name="ffn_block",
    )(h, w_gate, w_up, w_out, gain, bias)


def _conv_kernel(h_ref, w_b_ref, w_c_ref, w_u_ref, w_conv_ref, w_out_ref, gain_ref, bias_ref,
                 o_ref, carry_ref):
    @pl.when(pl.program_id(1) == 0)
    def _():
        carry_ref[...] = jnp.zeros_like(carry_ref)

    sub = h_ref.shape[1] // CONV_SUBTILES
    w = w_conv_ref[...]
    row = lax.broadcasted_iota(jnp.int32, (sub, 1), 0)
    prev = carry_ref[...]
    for s in range(CONV_SUBTILES):
        rows = slice(s * sub, (s + 1) * sub)
        x = h_ref[0, rows, :]
        xb = x.astype(BF16)
        bgate = _dot(xb, w_b_ref[...])
        u = _dot(xb, w_c_ref[...]) * _dot(xb, w_u_ref[...])
        last = prev[CONV_CARRY_ROWS - 1:CONV_CARRY_ROWS]
        u1 = jnp.where(row == 0, last, pltpu.roll(u, 1, 0))
        u2 = jnp.where(row == 0, prev[CONV_CARRY_ROWS - 2:CONV_CARRY_ROWS - 1],
                       jnp.where(row == 1, last, pltpu.roll(u, 2, 0)))
        prev = u[sub - CONV_CARRY_ROWS:]
        conv = w[0:1] * u2 + w[1:2] * u1 + w[2:3] * u
        mix = _dot((bgate * conv).astype(BF16), w_out_ref[...])
        o_ref[0, rows, :] = _layer_norm(DEEPNORM_ALPHA * x + mix, gain_ref[...], bias_ref[...])
    carry_ref[...] = prev


def _conv_block(h, w_b, w_c, w_u, w_conv, w_out, gain, bias):
    bsz, length, d = h.shape
    tm = CONV_TILE
    assert length % tm == 0 and CONV_WIDTH - 1 <= CONV_CARRY_ROWS <= tm // CONV_SUBTILES
    return pl.pallas_call(
        _conv_kernel,
        grid=(bsz, length // tm),
        in_specs=[
            pl.BlockSpec((1, tm, d), lambda b, t: (b, t, 0)),
            _resident((d, d)), _resident((d, d)), _resident((d, d)),
            _resident((CONV_WIDTH, d)), _resident((d, d)),
            _resident((1, d)), _resident((1, d)),
        ],
        out_specs=pl.BlockSpec((1, tm, d), lambda b, t: (b, t, 0)),
        out_shape=jax.ShapeDtypeStruct((bsz, length, d), F32),
        scratch_shapes=[pltpu.VMEM((CONV_CARRY_ROWS, d), F32)],
        compiler_params=pltpu.CompilerParams(
            dimension_semantics=("arbitrary", "arbitrary"), vmem_limit_bytes=VMEM_LIMIT_BYTES),
        name="conv_block",
    )(h, w_b, w_c, w_u, w_conv, w_out, gain, bias)


def _split3_bf16(a):
    hi = a.astype(BF16)
    r = a - hi.astype(F32)
    mid = r.astype(BF16)
    lo = (r - mid.astype(F32)).astype(BF16)
    return hi, mid, lo


def _gla_kernel(h_ref, w_q_ref, w_k_ref, w_v_ref, w_g_ref, w_gz_ref, w_gu_ref, b_gate_ref,
                norm_w_ref, w_out_ref, gain_ref, bias_ref, o_ref,
                state_ref, q_ref, k_ref, v_ref, la_ref, att_ref, qin_ref, p_ref, upd_ref,
                b_ref, qt_ref, kt_ref, kst_ref, kstT_ref):
    @pl.when(pl.program_id(1) == 0)
    def _():
        state_ref[...] = jnp.zeros_like(state_ref)

    x = h_ref[0]
    tm = x.shape[0]
    xb = x.astype(BF16)
    q_ref[...] = _dot(xb, w_q_ref[...]) * (GLA_HEAD_K ** -0.5)
    k_ref[...] = _dot(xb, w_k_ref[...])
    v_ref[...] = _dot(xb, w_v_ref[...]).astype(BF16)
    gz = _dot(xb, w_gz_ref[...])
    z = _dot(gz.astype(BF16), w_gu_ref[...]) + b_gate_ref[...]
    la_ref[...] = (jnp.minimum(z, 0.0) - jnp.log1p(jnp.exp(-jnp.abs(z)))) * (1.0 / GLA_GATE_NORMALIZER)

    c = GLA_CHUNK
    n_chunks = tm // c
    ri = lax.broadcasted_iota(jnp.int32, (c, c), 0)
    ci = lax.broadcasted_iota(jnp.int32, (c, c), 1)
    causal = ci <= ri
    tril = causal.astype(BF16)
    head_k = [slice(hd * GLA_HEAD_K, (hd + 1) * GLA_HEAD_K) for hd in range(GLA_HEADS)]
    head_v = [slice(hd * GLA_HEAD_V, (hd + 1) * GLA_HEAD_V) for hd in range(GLA_HEADS)]

    chunk_rows = [slice(ic * c, (ic + 1) * c) for ic in range(n_chunks)]
    for rows in chunk_rows:
        hi, mid, lo = _split3_bf16(la_ref[rows, :])
        b_ref[rows, :] = _dot(tril, hi) + _dot(tril, mid) + _dot(tril, lo)
    decays = []
    for rows in chunk_rows:
        b = b_ref[rows, :]
        b_mid = b[c // 2:c // 2 + 1]
        b_last = b[c - 1:c]
        q_t = q_ref[rows, :] * jnp.exp(b - b_mid)
        k_t = k_ref[rows, :] * jnp.exp(b_mid - b)
        qin_ref[rows, :] = (q_t * jnp.exp(b_mid)).astype(BF16)
        kst_ref[rows, :] = (k_t * jnp.exp(b_last - b_mid)).astype(BF16)
        qt_ref[rows, :] = q_t.astype(BF16)
        kt_ref[rows, :] = k_t.astype(BF16)
        decays.append(jnp.exp(b_last))
    for ic, rows in enumerate(chunk_rows):
        kstT_ref[ic] = jnp.transpose(kst_ref[rows, :])
    for ic, rows in enumerate(chunk_rows):
        for hd in range(GLA_HEADS):
            scores = lax.dot_general(qt_ref[rows, head_k[hd]], kt_ref[rows, head_k[hd]],
                                     (((1,), (1,)), ((), ())), preferred_element_type=F32)
            p_ref[hd, rows, :] = jnp.where(causal, scores, 0.0).astype(BF16)
            upd_ref[ic, hd] = _dot(kstT_ref[ic, head_k[hd], :], v_ref[rows, head_v[hd]])

    pad = [jnp.zeros_like(decays[0])] * (-n_chunks % 8)
    dec_rows = jnp.concatenate(decays + pad, axis=0)
    dec_cols = [jnp.transpose(dec_rows[:, head_k[hd]]) for hd in range(GLA_HEADS)]

    for ic in range(n_chunks):
        rows = slice(ic * c, (ic + 1) * c)
        for hd in range(GLA_HEADS):
            s_prev = state_ref[hd]
            att_ref[rows, head_v[hd]] = (_dot(p_ref[hd, rows, :], v_ref[rows, head_v[hd]])
                                         + _dot(qin_ref[rows, head_k[hd]], s_prev.astype(BF16)))
            state_ref[hd] = dec_cols[hd][:, ic:ic + 1] * s_prev + upd_ref[ic, hd]

    gate = _dot(xb, w_g_ref[...])
    gate = gate * jax.nn.sigmoid(gate)
    norm_w = norm_w_ref[...]
    heads = []
    for hd in range(GLA_HEADS):
        o = att_ref[:, head_v[hd]]
        o = o * lax.rsqrt(jnp.mean(o * o, axis=-1, keepdims=True) + RMS_EPS) * norm_w
        heads.append((o * gate[:, head_v[hd]]).astype(BF16))
    mix = _dot(jnp.concatenate(heads, axis=1), w_out_ref[...])
    o_ref[0] = _layer_norm(DEEPNORM_ALPHA * x + mix, gain_ref[...], bias_ref[...])


def _gla_block(h, w_q, w_k, w_v, w_g, w_gz, w_gu, b_gate, norm_w, w_out, gain, bias):
    bsz, length, d = h.shape
    tm = SEQ_TILE
    assert length % tm == 0 and tm % GLA_CHUNK == 0
    return pl.pallas_call(
        _gla_kernel,
        grid=(bsz, length // tm),
        in_specs=[
            pl.BlockSpec((1, tm, d), lambda b, t: (b, t, 0)),
            _resident((d, GLA_DK)), _resident((d, GLA_DK)), _resident((d, GLA_DV)), _resident((d, GLA_DV)),
            _resident((d, GLA_GATE_RANK)), _resident((GLA_GATE_RANK, GLA_DK)), _resident((1, GLA_DK)),
            _resident((1, GLA_HEAD_V)), _resident((GLA_DV, d)),
            _resident((1, d)), _resident((1, d)),
        ],
        out_specs=pl.BlockSpec((1, tm, d), lambda b, t: (b, t, 0)),
        out_shape=jax.ShapeDtypeStruct((bsz, length, d), F32),
        scratch_shapes=[
            pltpu.VMEM((GLA_HEADS, GLA_HEAD_K, GLA_HEAD_V), F32),
            pltpu.VMEM((tm, GLA_DK), F32),
            pltpu.VMEM((tm, GLA_DK), F32),
            pltpu.VMEM((tm, GLA_DV), BF16),
            pltpu.VMEM((tm, GLA_DK), F32),
            pltpu.VMEM((tm, GLA_DV), F32),
            pltpu.VMEM((tm, GLA_DK), BF16),
            pltpu.VMEM((GLA_HEADS, tm, GLA_CHUNK), BF16),
            pltpu.VMEM((tm // GLA_CHUNK, GLA_HEADS, GLA_HEAD_K, GLA_HEAD_V), F32),
            pltpu.VMEM((tm, GLA_DK), F32),
            pltpu.VMEM((tm, GLA_DK), BF16),
            pltpu.VMEM((tm, GLA_DK), BF16),
            pltpu.VMEM((tm, GLA_DK), BF16),
            pltpu.VMEM((tm // GLA_CHUNK, GLA_DK, GLA_CHUNK), BF16),
        ],
        compiler_params=pltpu.CompilerParams(
            dimension_semantics=("arbitrary", "arbitrary"), vmem_limit_bytes=VMEM_LIMIT_BYTES),
        name="gla_block",
    )(h, w_q, w_k, w_v, w_g, w_gz, w_gu, b_gate, norm_w, w_out, gain, bias)


def kernel(x, meta_tokens, ln_gain, ln_bias, ffn_w_in, ffn_w_out, conv_w_in, conv_w, conv_w_out,
           gla_w_in, gla_w_gate_up, gla_b_gate, gla_norm_w, gla_w_out):
    bsz, seq, d = x.shape
    length = N_META + seq
    meta = jnp.broadcast_to(meta_tokens[None].astype(x.dtype), (bsz, N_META, d))
    h = jnp.concatenate([meta, x], axis=1)

    def ffn(h, i, half):
        w_in = ffn_w_in[i, half].astype(BF16)
        out = _ffn_block(h.reshape(bsz * length, d), w_in[:, :D_FF], w_in[:, D_FF:],
                         ffn_w_out[i, half].astype(BF16),
                         ln_gain[i, 2 * half][None], ln_bias[i, 2 * half][None])
        return out.reshape(bsz, length, d)

    for i in range(DEPTH):
        h = ffn(h, i, 0)
        j = i // 2
        gain, bias = ln_gain[i, 1][None], ln_bias[i, 1][None]
        if i % 2 == 0:
            w_in = conv_w_in[j].astype(BF16)
            h = _conv_block(h, w_in[:, :d], w_in[:, d:2 * d], w_in[:, 2 * d:], conv_w[j],
                            conv_w_out[j].astype(BF16), gain, bias)
        else:
            w_in = gla_w_in[j].astype(BF16)
            o_q, o_k, o_v, o_g = GLA_DK, 2 * GLA_DK, 2 * GLA_DK + GLA_DV, 2 * GLA_DK + 2 * GLA_DV
            h = _gla_block(h, w_in[:, :o_q], w_in[:, o_q:o_k], w_in[:, o_k:o_v], w_in[:, o_v:o_g],
                           w_in[:, o_g:], gla_w_gate_up[j].astype(BF16), gla_b_gate[j][None],
                           gla_norm_w[j][None], gla_w_out[j].astype(BF16), gain, bias)
        h = ffn(h, i, 1)
    return h[:, N_META:]
```

```python
import jax
import jax.numpy as jnp
from jax import lax
from jax.experimental import pallas as pl
from jax.experimental.pallas import tpu as pltpu

D_MODEL = 1024
DEPTH = 2
N_META = 16
D_FF = 2816
CONV_WIDTH = 3
GLA_HEADS = 4
GLA_DK = D_MODEL // 2
GLA_DV = D_MODEL
GLA_HEAD_K = GLA_DK // GLA_HEADS
GLA_HEAD_V = GLA_DV // GLA_HEADS
GLA_GATE_RANK = 16
GLA_GATE_NORMALIZER = 16.0
GLA_CHUNK = 64
DEEPNORM_ALPHA = (2.0 * DEPTH) ** 0.25
LN_EPS = 1e-5
RMS_EPS = 1e-6

CONV_CARRY_ROWS = 8
SUB_TILE = 512
FFN_SUBTILES = 2
CONV_SUBTILES = 2
GLA_TILE = 512
VMEM_LIMIT_BYTES = 56 * 1024 * 1024

F32 = jnp.float32
BF16 = jnp.bfloat16


def _dot(a, b):
    return jnp.dot(a, b, preferred_element_type=F32)


def _layer_norm(y, gain, bias):
    mu = jnp.mean(y, axis=-1, keepdims=True)
    yc = y - mu
    var = jnp.mean(yc * yc, axis=-1, keepdims=True)
    return yc * lax.rsqrt(var + LN_EPS) * gain + bias


def _resident(shape):
    zeros = (0,) * len(shape)
    return pl.BlockSpec(shape, lambda *_: zeros, pipeline_mode=pl.Buffered(1))


def _ffn_rows(x, w_in_ref, w_out_ref, gain_ref, bias_ref):
    f = w_out_ref.shape[0]
    xb = x.astype(BF16)
    gate = _dot(xb, w_in_ref[:, :f])
    up = _dot(xb, w_in_ref[:, f:])
    act = (gate * jax.nn.sigmoid(gate) * up).astype(BF16)
    y = DEEPNORM_ALPHA * x + 0.5 * _dot(act, w_out_ref[...])
    return _layer_norm(y, gain_ref[...], bias_ref[...])


def _ffn_kernel(with_meta, *refs):
    if with_meta:
        x_ref, m_ref, w_in_ref, w_out_ref, gain_ref, bias_ref, ox_ref, om_ref = refs

        @pl.when(pl.program_id(0) == 0)
        def _():
            om_ref[...] = _ffn_rows(m_ref[...], w_in_ref, w_out_ref, gain_ref, bias_ref)
    else:
        x_ref, w_in_ref, w_out_ref, gain_ref, bias_ref, ox_ref = refs

    for s in range(x_ref.shape[0] // SUB_TILE):
        rows = slice(s * SUB_TILE, (s + 1) * SUB_TILE)
        ox_ref[rows, :] = _ffn_rows(x_ref[rows, :], w_in_ref, w_out_ref, gain_ref, bias_ref)


def _ffn_block(hx, hm, w_in, w_out, gain, bias):
    n, d = hx.shape
    f = w_out.shape[0]
    tm = SUB_TILE * FFN_SUBTILES
    assert n % tm == 0
    with_meta = hm is not None
    x_spec = pl.BlockSpec((tm, d), lambda i: (i, 0))
    w_specs = [_resident((d, 2 * f)), _resident((f, d)), _resident((1, d)), _resident((1, d))]
    params = pltpu.CompilerParams(dimension_semantics=("arbitrary",), vmem_limit_bytes=VMEM_LIMIT_BYTES)
    if not with_meta:
        return pl.pallas_call(
            lambda *refs: _ffn_kernel(False, *refs),
            grid=(n // tm,), in_specs=[x_spec] + w_specs, out_specs=x_spec,
            out_shape=jax.ShapeDtypeStruct((n, d), F32), compiler_params=params, name="ffn_block",
        )(hx, w_in, w_out, gain, bias), None
    return pl.pallas_call(
        lambda *refs: _ffn_kernel(True, *refs),
        grid=(n // tm,), in_specs=[x_spec, _resident((N_META, d))] + w_specs,
        out_specs=(x_spec, pl.BlockSpec((N_META, d), lambda i: (0, 0))),
        out_shape=(jax.ShapeDtypeStruct((n, d), F32), jax.ShapeDtypeStruct((N_META, d), F32)),
        compiler_params=params, name="ffn_meta_block",
    )(hx, hm, w_in, w_out, gain, bias)


def _conv_rows(x, prev, w_in_ref, w_conv_ref, w_out_ref, gain_ref, bias_ref):
    n, d = x.shape
    xb = x.astype(BF16)
    bgate = _dot(xb, w_in_ref[:, :d])
    u = _dot(xb, w_in_ref[:, d:2 * d]) * _dot(xb, w_in_ref[:, 2 * d:])
    row = lax.broadcasted_iota(jnp.int32, (n, 1), 0)
    last = prev[CONV_CARRY_ROWS - 1:CONV_CARRY_ROWS]
    u1 = jnp.where(row == 0, last, pltpu.roll(u, 1, 0))
    u2 = jnp.where(row == 0, prev[CONV_CARRY_ROWS - 2:CONV_CARRY_ROWS - 1],
                   jnp.where(row == 1, last, pltpu.roll(u, 2, 0)))
    w = w_conv_ref[...]
    conv = w[0:1] * u2 + w[1:2] * u1 + w[2:3] * u
    mix = _dot((bgate * conv).astype(BF16), w_out_ref[...])
    out = _layer_norm(DEEPNORM_ALPHA * x + mix, gain_ref[...], bias_ref[...])
    return out, u[n - CONV_CARRY_ROWS:]


def _conv_kernel(x_ref, m_ref, w_in_ref, w_conv_ref, w_out_ref, gain_ref, bias_ref,
                 ox_ref, om_ref, carry_ref, meta_carry_ref):
    weights = (w_in_ref, w_conv_ref, w_out_ref, gain_ref, bias_ref)

    @pl.when((pl.program_id(0) == 0) & (pl.program_id(1) == 0))
    def _():
        zeros = jnp.zeros((CONV_CARRY_ROWS, x_ref.shape[2]), F32)
        om_ref[...], meta_carry_ref[...] = _conv_rows(m_ref[...], zeros, *weights)

    @pl.when(pl.program_id(1) == 0)
    def _():
        carry_ref[...] = meta_carry_ref[...]

    prev = carry_ref[...]
    for s in range(x_ref.shape[1] // SUB_TILE):
        rows = slice(s * SUB_TILE, (s + 1) * SUB_TILE)
        ox_ref[0, rows, :], prev = _conv_rows(x_ref[0, rows, :], prev, *weights)
    carry_ref[...] = prev


def _conv_block(hx, hm, w_in, w_conv, w_out, gain, bias):
    bsz, seq, d = hx.shape
    tm = SUB_TILE * CONV_SUBTILES
    assert seq % tm == 0 and CONV_WIDTH - 1 <= CONV_CARRY_ROWS <= N_META
    x_spec = pl.BlockSpec((1, tm, d), lambda b, t: (b, t, 0))
    return pl.pallas_call(
        _conv_kernel,
        grid=(bsz, seq // tm),
        in_specs=[x_spec, _resident((N_META, d)), _resident((d, 3 * d)), _resident((CONV_WIDTH, d)),
                  _resident((d, d)), _resident((1, d)), _resident((1, d))],
        out_specs=(x_spec, pl.BlockSpec((N_META, d), lambda b, t: (0, 0))),
        out_shape=(jax.ShapeDtypeStruct((bsz, seq, d), F32), jax.ShapeDtypeStruct((N_META, d), F32)),
        scratch_shapes=[pltpu.VMEM((CONV_CARRY_ROWS, d), F32), pltpu.VMEM((CONV_CARRY_ROWS, d), F32)],
        compiler_params=pltpu.CompilerParams(
            dimension_semantics=("arbitrary", "arbitrary"), vmem_limit_bytes=VMEM_LIMIT_BYTES),
        name="conv_block",
    )(hx, hm, w_in, w_conv, w_out, gain, bias)


def _split3_bf16(a):
    hi = a.astype(BF16)
    r = a - hi.astype(F32)
    mid = r.astype(BF16)
    lo = (r - mid.astype(F32)).astype(BF16)
    return hi, mid, lo


def _gla_rows(x, c, state_ref, weights, scratch):
    (w_in_ref, w_gu_ref, b_gate_ref, norm_w_ref, w_out_ref, gain_ref, bias_ref) = weights
    (q_ref, k_ref, v_ref, la_ref, b_ref, att_ref, qin_ref, qt_ref, kt_ref, kst_ref, kstT_ref,
     p_ref, upd_ref) = scratch
    n = x.shape[0]
    n_chunks = n // c
    all_rows = slice(0, n)
    o_k, o_v, o_g, o_gz = GLA_DK, 2 * GLA_DK, 2 * GLA_DK + GLA_DV, 2 * GLA_DK + 2 * GLA_DV
    xb = x.astype(BF16)
    q_ref[all_rows, :] = _dot(xb, w_in_ref[:, :o_k]) * (GLA_HEAD_K ** -0.5)
    k_ref[all_rows, :] = _dot(xb, w_in_ref[:, o_k:o_v])
    v_ref[all_rows, :] = _dot(xb, w_in_ref[:, o_v:o_g]).astype(BF16)
    gz = _dot(xb, w_in_ref[:, o_gz:])
    z = _dot(gz.astype(BF16), w_gu_ref[...]) + b_gate_ref[...]
    la_ref[all_rows, :] = (jnp.minimum(z, 0.0) - jnp.log1p(jnp.exp(-jnp.abs(z)))) * (1.0 / GLA_GATE_NORMALIZER)

    ri = lax.broadcasted_iota(jnp.int32, (c, c), 0)
    ci = lax.broadcasted_iota(jnp.int32, (c, c), 1)
    causal = ci <= ri
    tril = causal.astype(BF16)
    head_k = [slice(hd * GLA_HEAD_K, (hd + 1) * GLA_HEAD_K) for hd in range(GLA_HEADS)]
    head_v = [slice(hd * GLA_HEAD_V, (hd + 1) * GLA_HEAD_V) for hd in range(GLA_HEADS)]
    chunk_rows = [slice(ic * c, (ic + 1) * c) for ic in range(n_chunks)]

    for rows in chunk_rows:
        hi, mid, lo = _split3_bf16(la_ref[rows, :])
        b_ref[rows, :] = _dot(tril, hi) + _dot(tril, mid) + _dot(tril, lo)
    decays = []
    for rows in chunk_rows:
        b = b_ref[rows, :]
        b_mid = b[c // 2:c // 2 + 1]
        b_last = b[c - 1:c]
        q_t = q_ref[rows, :] * jnp.exp(b - b_mid)
        k_t = k_ref[rows, :] * jnp.exp(b_mid - b)
        qin_ref[rows, :] = (q_t * jnp.exp(b_mid)).astype(BF16)
        kst_ref[rows, :] = (k_t * jnp.exp(b_last - b_mid)).astype(BF16)
        qt_ref[rows, :] = q_t.astype(BF16)
        kt_ref[rows, :] = k_t.astype(BF16)
        decays.append(jnp.exp(b_last))
    for ic, rows in enumerate(chunk_rows):
        kstT_ref[ic, :, 0:c] = jnp.transpose(kst_ref[rows, :])
    for ic, rows in enumerate(chunk_rows):
        for hd in range(GLA_HEADS):
            scores = lax.dot_general(qt_ref[rows, head_k[hd]], kt_ref[rows, head_k[hd]],
                                     (((1,), (1,)), ((), ())), preferred_element_type=F32)
            p_ref[hd, rows, 0:c] = jnp.where(causal, scores, 0.0).astype(BF16)
            upd_ref[ic, hd] = _dot(kstT_ref[ic, head_k[hd], 0:c], v_ref[rows, head_v[hd]])

    pad = [jnp.zeros_like(decays[0])] * (-n_chunks % 8)
    dec_rows = jnp.concatenate(decays + pad, axis=0)
    dec_cols = [jnp.transpose(dec_rows[:, head_k[hd]]) for hd in range(GLA_HEADS)]

    for ic, rows in enumerate(chunk_rows):
        for hd in range(GLA_HEADS):
            s_prev = state_ref[hd]
            att_ref[rows, head_v[hd]] = (_dot(p_ref[hd, rows, 0:c], v_ref[rows, head_v[hd]])
                                         + _dot(qin_ref[rows, head_k[hd]], s_prev.astype(BF16)))
            state_ref[hd] = dec_cols[hd][:, ic:ic + 1] * s_prev + upd_ref[ic, hd]

    gate = _dot(xb, w_in_ref[:, o_g:o_gz])
    gate = gate * jax.nn.sigmoid(gate)
    norm_w = norm_w_ref[...]
    heads = []
    for hd in range(GLA_HEADS):
        o = att_ref[all_rows, head_v[hd]]
        o = o * lax.rsqrt(jnp.mean(o * o, axis=-1, keepdims=True) + RMS_EPS) * norm_w
        heads.append((o * gate[:, head_v[hd]]).astype(BF16))
    mix = _dot(jnp.concatenate(heads, axis=1), w_out_ref[...])
    return _layer_norm(DEEPNORM_ALPHA * x + mix, gain_ref[...], bias_ref[...])


def _gla_kernel(x_ref, m_ref, w_in_ref, w_gu_ref, b_gate_ref, norm_w_ref, w_out_ref, gain_ref, bias_ref,
                ox_ref, om_ref, state_ref, meta_state_ref, *scratch):
    weights = (w_in_ref, w_gu_ref, b_gate_ref, norm_w_ref, w_out_ref, gain_ref, bias_ref)

    @pl.when((pl.program_id(0) == 0) & (pl.program_id(1) == 0))
    def _():
        meta_state_ref[...] = jnp.zeros_like(meta_state_ref)
        om_ref[...] = _gla_rows(m_ref[...], N_META, meta_state_ref, weights, scratch)

    @pl.when(pl.program_id(1) == 0)
    def _():
        state_ref[...] = meta_state_ref[...]

    ox_ref[0] = _gla_rows(x_ref[0], GLA_CHUNK, state_ref, weights, scratch)


def _gla_block(hx, hm, w_in, w_gu, b_gate, norm_w, w_out, gain, bias):
    bsz, seq, d = hx.shape
    tm = GLA_TILE
    assert seq % tm == 0 and tm % GLA_CHUNK == 0 and N_META <= GLA_CHUNK
    n_chunks = tm // GLA_CHUNK
    x_spec = pl.BlockSpec((1, tm, d), lambda b, t: (b, t, 0))
    state = pltpu.VMEM((GLA_HEADS, GLA_HEAD_K, GLA_HEAD_V), F32)
    return pl.pallas_call(
        _gla_kernel,
        grid=(bsz, seq // tm),
        in_specs=[x_spec, _resident((N_META, d)), _resident(w_in.shape), _resident((GLA_GATE_RANK, GLA_DK)),
                  _resident((1, GLA_DK)), _resident((1, GLA_HEAD_V)), _resident((GLA_DV, d)),
                  _resident((1, d)), _resident((1, d))],
        out_specs=(x_spec, pl.BlockSpec((N_META, d), lambda b, t: (0, 0))),
        out_shape=(jax.ShapeDtypeStruct((bsz, seq, d), F32), jax.ShapeDtypeStruct((N_META, d), F32)),
        scratch_shapes=[
            state, state,
            pltpu.VMEM((tm, GLA_DK), F32),
            pltpu.VMEM((tm, GLA_DK), F32),
            pltpu.VMEM((tm, GLA_DV), BF16),
            pltpu.VMEM((tm, GLA_DK), F32),
            pltpu.VMEM((tm, GLA_DK), F32),
            pltpu.VMEM((tm, GLA_DV), F32),
            pltpu.VMEM((tm, GLA_DK), BF16),
            pltpu.VMEM((tm, GLA_DK), BF16),
            pltpu.VMEM((tm, GLA_DK), BF16),
            pltpu.VMEM((tm, GLA_DK), BF16),
            pltpu.VMEM((n_chunks, GLA_DK, GLA_CHUNK), BF16),
            pltpu.VMEM((GLA_HEADS, tm, GLA_CHUNK), BF16),
            pltpu.VMEM((n_chunks, GLA_HEADS, GLA_HEAD_K, GLA_HEAD_V), F32),
        ],
        compiler_params=pltpu.CompilerParams(
            dimension_semantics=("arbitrary", "arbitrary"), vmem_limit_bytes=VMEM_LIMIT_BYTES),
        name="gla_block",
    )(hx, hm, w_in, w_gu, b_gate, norm_w, w_out, gain, bias)


def kernel(x, meta_tokens, ln_gain, ln_bias, ffn_w_in, ffn_w_out, conv_w_in, conv_w, conv_w_out,
           gla_w_in, gla_w_gate_up, gla_b_gate, gla_norm_w, gla_w_out):
    bsz, seq, d = x.shape
    hx = x
    hm = meta_tokens.astype(x.dtype)

    def ffn(hx, hm, i, half):
        ox, om = _ffn_block(hx.reshape(bsz * seq, d), hm, ffn_w_in[i, half].astype(BF16),
                            ffn_w_out[i, half].astype(BF16),
                            ln_gain[i, 2 * half][None], ln_bias[i, 2 * half][None])
        return ox.reshape(bsz, seq, d), om

    for i in range(DEPTH):
        hx, hm = ffn(hx, hm, i, 0)
        j = i // 2
        gain, bias = ln_gain[i, 1][None], ln_bias[i, 1][None]
        if i % 2 == 0:
            hx, hm = _conv_block(hx, hm, conv_w_in[j].astype(BF16), conv_w[j],
                                 conv_w_out[j].astype(BF16), gain, bias)
        else:
            hx, hm = _gla_block(hx, hm, gla_w_in[j].astype(BF16), gla_w_gate_up[j].astype(BF16),
                                gla_b_gate[j][None], gla_norm_w[j][None], gla_w_out[j].astype(BF16), gain, bias)
        hx, hm = ffn(hx, hm if i + 1 < DEPTH else None, i, 1)
    return hx
```

```python
import jax
import jax.numpy as jnp
from jax import lax
from jax.experimental import pallas as pl
from jax.experimental.pallas import tpu as pltpu

D_MODEL = 1024
DEPTH = 2
N_META = 16
D_FF = 2816
CONV_WIDTH = 3
GLA_HEADS = 4
GLA_DK = D_MODEL // 2
GLA_DV = D_MODEL
GLA_HEAD_K = GLA_DK // GLA_HEADS
GLA_HEAD_V = GLA_DV // GLA_HEADS
GLA_GATE_RANK = 16
GLA_GATE_NORMALIZER = 16.0
GLA_CHUNK = 64
DEEPNORM_ALPHA = (2.0 * DEPTH) ** 0.25
LN_EPS = 1e-5
RMS_EPS = 1e-6

CONV_CARRY_ROWS = 8
SUB_TILE = 512
CONV_SUBTILES = 2
GLA_TILE = 512
VMEM_LIMIT_BYTES = 56 * 1024 * 1024

F32 = jnp.float32
BF16 = jnp.bfloat16


def _dot(a, b):
    return jnp.dot(a, b, preferred_element_type=F32)


def _layer_norm(y, gain, bias):
    mu = jnp.mean(y, axis=-1, keepdims=True)
    yc = y - mu
    var = jnp.mean(yc * yc, axis=-1, keepdims=True)
    return yc * lax.rsqrt(var + LN_EPS) * gain + bias


def _resident(shape):
    zeros = (0,) * len(shape)
    return pl.BlockSpec(shape, lambda *_: zeros, pipeline_mode=pl.Buffered(1))


def _ffn_residual(x, w_in_ref, w_out_ref):
    f = w_out_ref.shape[0]
    xb = x.astype(BF16)
    gate = _dot(xb, w_in_ref[:, :f])
    up = _dot(xb, w_in_ref[:, f:])
    act = (gate * jax.nn.sigmoid(gate) * up).astype(BF16)
    return DEEPNORM_ALPHA * x + 0.5 * _dot(act, w_out_ref[...])


def _ffn_kernel(with_meta, n_tiles, *refs):
    if with_meta:
        x_ref, m_ref, w_in_ref, w_out_ref, gain_ref, bias_ref, ox_ref, om_ref, y_ref = refs
    else:
        x_ref, w_in_ref, w_out_ref, gain_ref, bias_ref, ox_ref, y_ref = refs
    i = pl.program_id(0)

    def normalise_previous():
        ox_ref[...] = _layer_norm(y_ref[...], gain_ref[...], bias_ref[...])

    def form_current():
        y_ref[...] = _ffn_residual(x_ref[...], w_in_ref, w_out_ref)

    @pl.when(i == 0)
    def _():
        if with_meta:
            om_ref[...] = _layer_norm(_ffn_residual(m_ref[...], w_in_ref, w_out_ref),
                                      gain_ref[...], bias_ref[...])
        form_current()

    @pl.when((i > 0) & (i < n_tiles))
    def _():
        normalise_previous()
        form_current()

    @pl.when(i == n_tiles)
    def _():
        normalise_previous()


def _ffn_block(hx, hm, w_in, w_out, gain, bias, sel):
    n, d = hx.shape
    f = w_out.shape[2]
    tm = SUB_TILE
    assert n % tm == 0
    n_tiles = n // tm
    with_meta = hm is not None
    x_spec = pl.BlockSpec((tm, d), lambda i: (jnp.minimum(i, n_tiles - 1), 0))
    o_spec = pl.BlockSpec((tm, d), lambda i: (jnp.maximum(i - 1, 0), 0))
    w_specs = [
        pl.BlockSpec((None, None, d, 2 * f), lambda i: (*sel, 0, 0), pipeline_mode=pl.Buffered(1)),
        pl.BlockSpec((None, None, f, d), lambda i: (*sel, 0, 0), pipeline_mode=pl.Buffered(1)),
        _resident((1, d)), _resident((1, d)),
    ]
    params = pltpu.CompilerParams(dimension_semantics=("arbitrary",), vmem_limit_bytes=VMEM_LIMIT_BYTES)
    scratch = [pltpu.VMEM((tm, d), F32)]
    if not with_meta:
        return pl.pallas_call(
            lambda *refs: _ffn_kernel(False, n_tiles, *refs),
            grid=(n_tiles + 1,), in_specs=[x_spec] + w_specs, out_specs=o_spec,
            out_shape=jax.ShapeDtypeStruct((n, d), F32), scratch_shapes=scratch,
            compiler_params=params, name="ffn_block",
        )(hx, w_in, w_out, gain, bias), None
    return pl.pallas_call(
        lambda *refs: _ffn_kernel(True, n_tiles, *refs),
        grid=(n_tiles + 1,), in_specs=[x_spec, _resident((N_META, d))] + w_specs,
        out_specs=(o_spec, pl.BlockSpec((N_META, d), lambda i: (0, 0))),
        out_shape=(jax.ShapeDtypeStruct((n, d), F32), jax.ShapeDtypeStruct((N_META, d), F32)),
        scratch_shapes=scratch, compiler_params=params, name="ffn_meta_block",
    )(hx, hm, w_in, w_out, gain, bias)


def _conv_rows(x, prev, w_in_ref, w_conv_ref, w_out_ref, gain_ref, bias_ref):
    n, d = x.shape
    xb = x.astype(BF16)
    u = _dot(xb, w_in_ref[:, d:2 * d]) * _dot(xb, w_in_ref[:, 2 * d:])
    bgate = _dot(xb, w_in_ref[:, :d])
    row = lax.broadcasted_iota(jnp.int32, (n, 1), 0)
    last = prev[CONV_CARRY_ROWS - 1:CONV_CARRY_ROWS]
    u1 = jnp.where(row == 0, last, pltpu.roll(u, 1, 0))
    u2 = jnp.where(row == 0, prev[CONV_CARRY_ROWS - 2:CONV_CARRY_ROWS - 1],
                   jnp.where(row == 1, last, pltpu.roll(u, 2, 0)))
    w = w_conv_ref[...]
    conv = w[0:1] * u2 + w[1:2] * u1 + w[2:3] * u
    mix = _dot((bgate * conv).astype(BF16), w_out_ref[...])
    out = _layer_norm(DEEPNORM_ALPHA * x + mix, gain_ref[...], bias_ref[...])
    return out, u[n - CONV_CARRY_ROWS:]


def _conv_kernel(x_ref, m_ref, w_in_ref, w_conv_ref, w_out_ref, gain_ref, bias_ref,
                 ox_ref, om_ref, carry_ref, meta_carry_ref):
    weights = (w_in_ref, w_conv_ref, w_out_ref, gain_ref, bias_ref)

    @pl.when((pl.program_id(0) == 0) & (pl.program_id(1) == 0))
    def _():
        zeros = jnp.zeros((CONV_CARRY_ROWS, x_ref.shape[2]), F32)
        om_ref[...], meta_carry_ref[...] = _conv_rows(m_ref[...], zeros, *weights)

    @pl.when(pl.program_id(1) == 0)
    def _():
        carry_ref[...] = meta_carry_ref[...]

    prev = carry_ref[...]
    for s in range(x_ref.shape[1] // SUB_TILE):
        rows = slice(s * SUB_TILE, (s + 1) * SUB_TILE)
        ox_ref[0, rows, :], prev = _conv_rows(x_ref[0, rows, :], prev, *weights)
    carry_ref[...] = prev


def _conv_block(hx, hm, w_in, w_conv, w_out, gain, bias):
    bsz, seq, d = hx.shape
    tm = SUB_TILE * CONV_SUBTILES
    assert seq % tm == 0 and CONV_WIDTH - 1 <= CONV_CARRY_ROWS <= N_META
    x_spec = pl.BlockSpec((1, tm, d), lambda b, t: (b, t, 0))
    return pl.pallas_call(
        _conv_kernel,
        grid=(bsz, seq // tm),
        in_specs=[x_spec, _resident((N_META, d)), _resident((d, 3 * d)), _resident((CONV_WIDTH, d)),
                  _resident((d, d)), _resident((1, d)), _resident((1, d))],
        out_specs=(x_spec, pl.BlockSpec((N_META, d), lambda b, t: (0, 0))),
        out_shape=(jax.ShapeDtypeStruct((bsz, seq, d), F32), jax.ShapeDtypeStruct((N_META, d), F32)),
        scratch_shapes=[pltpu.VMEM((CONV_CARRY_ROWS, d), F32), pltpu.VMEM((CONV_CARRY_ROWS, d), F32)],
        compiler_params=pltpu.CompilerParams(
            dimension_semantics=("arbitrary", "arbitrary"), vmem_limit_bytes=VMEM_LIMIT_BYTES),
        name="conv_block",
    )(hx, hm, w_in, w_conv, w_out, gain, bias)


def _split3_bf16(a):
    hi = a.astype(BF16)
    r = a - hi.astype(F32)
    mid = r.astype(BF16)
    lo = (r - mid.astype(F32)).astype(BF16)
    return hi, mid, lo


def _gla_rows(x, c, state_ref, weights, scratch):
    (w_in_ref, w_gu_ref, b_gate_ref, norm_w_ref, w_out_ref, gain_ref, bias_ref) = weights
    (q_ref, k_ref, v_ref, la_ref, b_ref, att_ref, qin_ref, qt_ref, kt_ref, kst_ref, kstT_ref,
     p_ref, upd_ref, gate_ref) = scratch
    n = x.shape[0]
    n_chunks = n // c
    all_rows = slice(0, n)
    o_k, o_v, o_g, o_gz = GLA_DK, 2 * GLA_DK, 2 * GLA_DK + GLA_DV, 2 * GLA_DK + 2 * GLA_DV
    xb = x.astype(BF16)
    gz = _dot(xb, w_in_ref[:, o_gz:])
    q_ref[all_rows, :] = _dot(xb, w_in_ref[:, :o_k]) * (GLA_HEAD_K ** -0.5)
    k_ref[all_rows, :] = _dot(xb, w_in_ref[:, o_k:o_v])
    z = _dot(gz.astype(BF16), w_gu_ref[...]) + b_gate_ref[...]
    v_ref[all_rows, :] = _dot(xb, w_in_ref[:, o_v:o_g]).astype(BF16)
    la_ref[all_rows, :] = (jnp.minimum(z, 0.0) - jnp.log1p(jnp.exp(-jnp.abs(z)))) * (1.0 / GLA_GATE_NORMALIZER)

    ri = lax.broadcasted_iota(jnp.int32, (c, c), 0)
    ci = lax.broadcasted_iota(jnp.int32, (c, c), 1)
    causal = ci <= ri
    tril = causal.astype(BF16)
    head_k = [slice(hd * GLA_HEAD_K, (hd + 1) * GLA_HEAD_K) for hd in range(GLA_HEADS)]
    head_v = [slice(hd * GLA_HEAD_V, (hd + 1) * GLA_HEAD_V) for hd in range(GLA_HEADS)]
    chunk_rows = [slice(ic * c, (ic + 1) * c) for ic in range(n_chunks)]

    for rows in chunk_rows:
        hi, mid, lo = _split3_bf16(la_ref[rows, :])
        b_ref[rows, :] = _dot(tril, hi) + _dot(tril, mid) + _dot(tril, lo)
    gate = _dot(xb, w_in_ref[:, o_g:o_gz])
    gate_ref[all_rows, :] = gate * jax.nn.sigmoid(gate)
    decays = []
    for rows in chunk_rows:
        b = b_ref[rows, :]
        b_mid = b[c // 2:c // 2 + 1]
        b_last = b[c - 1:c]
        q_t = q_ref[rows, :] * jnp.exp(b - b_mid)
        k_t = k_ref[rows, :] * jnp.exp(b_mid - b)
        qin_ref[rows, :] = (q_t * jnp.exp(b_mid)).astype(BF16)
        kst_ref[rows, :] = (k_t * jnp.exp(b_last - b_mid)).astype(BF16)
        qt_ref[rows, :] = q_t.astype(BF16)
        kt_ref[rows, :] = k_t.astype(BF16)
        decays.append(jnp.exp(b_last))
    for ic, rows in enumerate(chunk_rows):
        kstT_ref[ic, :, 0:c] = jnp.transpose(kst_ref[rows, :])
    for ic, rows in enumerate(chunk_rows):
        for hd in range(GLA_HEADS):
            scores = lax.dot_general(qt_ref[rows, head_k[hd]], kt_ref[rows, head_k[hd]],
                                     (((1,), (1,)), ((), ())), preferred_element_type=F32)
            p_ref[hd, rows, 0:c] = jnp.where(causal, scores, 0.0).astype(BF16)
            upd_ref[ic, hd] = _dot(kstT_ref[ic, head_k[hd], 0:c], v_ref[rows, head_v[hd]])

    pad = [jnp.zeros_like(decays[0])] * (-n_chunks % 8)
    dec_rows = jnp.concatenate(decays + pad, axis=0)
    dec_cols = [jnp.transpose(dec_rows[:, head_k[hd]]) for hd in range(GLA_HEADS)]

    for ic, rows in enumerate(chunk_rows):
        for hd in range(GLA_HEADS):
            s_prev = state_ref[hd]
            att_ref[rows, head_v[hd]] = (_dot(p_ref[hd, rows, 0:c], v_ref[rows, head_v[hd]])
                                         + _dot(qin_ref[rows, head_k[hd]], s_prev.astype(BF16)))
            state_ref[hd] = dec_cols[hd][:, ic:ic + 1] * s_prev + upd_ref[ic, hd]

    norm_w = norm_w_ref[...]
    heads = []
    for hd in range(GLA_HEADS):
        o = att_ref[all_rows, head_v[hd]]
        o = o * lax.rsqrt(jnp.mean(o * o, axis=-1, keepdims=True) + RMS_EPS) * norm_w
        heads.append((o * gate_ref[all_rows, head_v[hd]]).astype(BF16))
    mix = _dot(jnp.concatenate(heads, axis=1), w_out_ref[...])
    return _layer_norm(DEEPNORM_ALPHA * x + mix, gain_ref[...], bias_ref[...])


def _gla_kernel(x_ref, m_ref, w_in_ref, w_gu_ref, b_gate_ref, norm_w_ref, w_out_ref, gain_ref, bias_ref,
                ox_ref, om_ref, state_ref, meta_state_ref, *scratch):
    weights = (w_in_ref, w_gu_ref, b_gate_ref, norm_w_ref, w_out_ref, gain_ref, bias_ref)

    @pl.when((pl.program_id(0) == 0) & (pl.program_id(1) == 0))
    def _():
        meta_state_ref[...] = jnp.zeros_like(meta_state_ref)
        om_ref[...] = _gla_rows(m_ref[...], N_META, meta_state_ref, weights, scratch)

    @pl.when(pl.program_id(1) == 0)
    def _():
        state_ref[...] = meta_state_ref[...]

    ox_ref[0] = _gla_rows(x_ref[0], GLA_CHUNK, state_ref, weights, scratch)


def _gla_block(hx, hm, w_in, w_gu, b_gate, norm_w, w_out, gain, bias):
    bsz, seq, d = hx.shape
    tm = GLA_TILE
    assert seq % tm == 0 and tm % GLA_CHUNK == 0 and N_META <= GLA_CHUNK
    n_chunks = tm // GLA_CHUNK
    x_spec = pl.BlockSpec((1, tm, d), lambda b, t: (b, t, 0))
    state = pltpu.VMEM((GLA_HEADS, GLA_HEAD_K, GLA_HEAD_V), F32)
    return pl.pallas_call(
        _gla_kernel,
        grid=(bsz, seq // tm),
        in_specs=[x_spec, _resident((N_META, d)), _resident(w_in.shape), _resident((GLA_GATE_RANK, GLA_DK)),
                  _resident((1, GLA_DK)), _resident((1, GLA_HEAD_V)), _resident((GLA_DV, d)),
                  _resident((1, d)), _resident((1, d))],
        out_specs=(x_spec, pl.BlockSpec((N_META, d), lambda b, t: (0, 0))),
        out_shape=(jax.ShapeDtypeStruct((bsz, seq, d), F32), jax.ShapeDtypeStruct((N_META, d), F32)),
        scratch_shapes=[
            state, state,
            pltpu.VMEM((tm, GLA_DK), F32),
            pltpu.VMEM((tm, GLA_DK), F32),
            pltpu.VMEM((tm, GLA_DV), BF16),
            pltpu.VMEM((tm, GLA_DK), F32),
            pltpu.VMEM((tm, GLA_DK), F32),
            pltpu.VMEM((tm, GLA_DV), F32),
            pltpu.VMEM((tm, GLA_DK), BF16),
            pltpu.VMEM((tm, GLA_DK), BF16),
            pltpu.VMEM((tm, GLA_DK), BF16),
            pltpu.VMEM((tm, GLA_DK), BF16),
            pltpu.VMEM((n_chunks, GLA_DK, GLA_CHUNK), BF16),
            pltpu.VMEM((GLA_HEADS, tm, GLA_CHUNK), BF16),
            pltpu.VMEM((n_chunks, GLA_HEADS, GLA_HEAD_K, GLA_HEAD_V), F32),
            pltpu.VMEM((tm, GLA_DV), F32),
        ],
        compiler_params=pltpu.CompilerParams(
            dimension_semantics=("arbitrary", "arbitrary"), vmem_limit_bytes=VMEM_LIMIT_BYTES),
        name="gla_block",
    )(hx, hm, w_in, w_gu, b_gate, norm_w, w_out, gain, bias)


def kernel(x, meta_tokens, ln_gain, ln_bias, ffn_w_in, ffn_w_out, conv_w_in, conv_w, conv_w_out,
           gla_w_in, gla_w_gate_up, gla_b_gate, gla_norm_w, gla_w_out):
    bsz, seq, d = x.shape
    hx = x
    hm = meta_tokens.astype(x.dtype)

    def ffn(hx, hm, i, half):
        ox, om = _ffn_block(hx.reshape(bsz * seq, d), hm, ffn_w_in, ffn_w_out,
                            ln_gain[i, 2 * half][None], ln_bias[i, 2 * half][None], (i, half))
        return ox.reshape(bsz, seq, d), om

    for i in range(DEPTH):
        hx, hm = ffn(hx, hm, i, 0)
        j = i // 2
        gain, bias = ln_gain[i, 1][None], ln_bias[i, 1][None]
        if i % 2 == 0:
            hx, hm = _conv_block(hx, hm, conv_w_in[j], conv_w[j], conv_w_out[j], gain, bias)
        else:
            hx, hm = _gla_block(hx, hm, gla_w_in[j], gla_w_gate_up[j].astype(BF16),
                                gla_b_gate[j][None], gla_norm_w[j][None], gla_w_out[j], gain, bias)
        hx, hm = ffn(hx, hm if i + 1 < DEPTH else None, i, 1)
    return hx
```

```python
import jax
import jax.numpy as jnp
from jax import lax
from jax.experimental import pallas as pl
from jax.experimental.pallas import tpu as pltpu

D_MODEL = 1024
DEPTH = 2
N_META = 16
D_FF = 2816
CONV_WIDTH = 3
GLA_HEADS = 4
GLA_DK = D_MODEL // 2
GLA_DV = D_MODEL
GLA_HEAD_K = GLA_DK // GLA_HEADS
GLA_HEAD_V = GLA_DV // GLA_HEADS
GLA_GATE_RANK = 16
GLA_GATE_NORMALIZER = 16.0
GLA_CHUNK = 64
DEEPNORM_ALPHA = (2.0 * DEPTH) ** 0.25
LN_EPS = 1e-5
RMS_EPS = 1e-6

CONV_CARRY_ROWS = 8
SUB_TILE = 512
CONV_SUBTILES = 2
GLA_TILE = 512
VMEM_LIMIT_BYTES = 56 * 1024 * 1024

F32 = jnp.float32
BF16 = jnp.bfloat16


def _dot(a, b):
    return jnp.dot(a, b, preferred_element_type=F32)


def _layer_norm(y, gain, bias):
    mu = jnp.mean(y, axis=-1, keepdims=True)
    yc = y - mu
    var = jnp.mean(yc * yc, axis=-1, keepdims=True)
    return yc * lax.rsqrt(var + LN_EPS) * gain + bias


def _resident(shape):
    zeros = (0,) * len(shape)
    return pl.BlockSpec(shape, lambda *_: zeros, pipeline_mode=pl.Buffered(1))


def _ffn_residual(x, w_in_ref, w_out_ref, before_gate=None, before_up=None, before_out=None):
    f = w_out_ref.shape[0]
    xb = x.astype(BF16)
    if before_gate is not None:
        before_gate()
    gate = _dot(xb, w_in_ref[:, :f])
    if before_up is not None:
        before_up()
    up = _dot(xb, w_in_ref[:, f:])
    act = (gate * jax.nn.sigmoid(gate) * up).astype(BF16)
    if before_out is not None:
        before_out()
    return DEEPNORM_ALPHA * x + 0.5 * _dot(act, w_out_ref[...])


def _ffn_kernel(with_meta, n_tiles, sel, *refs):
    if with_meta:
        (x_ref, m_ref, w_in_hbm, w_out_hbm, gain_ref, bias_ref, ox_ref, om_ref,
         y_ref, w_in_ref, w_out_ref, sem) = refs
    else:
        x_ref, w_in_hbm, w_out_hbm, gain_ref, bias_ref, ox_ref, y_ref, w_in_ref, w_out_ref, sem = refs
    i = pl.program_id(0)
    f = w_out_ref.shape[0]
    layer, half = sel
    copies = (
        pltpu.make_async_copy(w_in_hbm.at[layer, half, :, pl.ds(0, f)], w_in_ref.at[:, pl.ds(0, f)], sem.at[0]),
        pltpu.make_async_copy(w_in_hbm.at[layer, half, :, pl.ds(f, f)], w_in_ref.at[:, pl.ds(f, f)], sem.at[1]),
        pltpu.make_async_copy(w_out_hbm.at[layer, half], w_out_ref, sem.at[2]),
    )

    def normalise_previous():
        ox_ref[...] = _layer_norm(y_ref[...], gain_ref[...], bias_ref[...])

    def form_current(*hooks):
        y_ref[...] = _ffn_residual(x_ref[...], w_in_ref, w_out_ref, *hooks)

    @pl.when(i == 0)
    def _():
        for copy in copies:
            copy.start()
        form_current(*(copy.wait for copy in copies))
        if with_meta:
            om_ref[...] = _layer_norm(_ffn_residual(m_ref[...], w_in_ref, w_out_ref),
                                      gain_ref[...], bias_ref[...])

    @pl.when((i > 0) & (i < n_tiles))
    def _():
        normalise_previous()
        form_current()

    @pl.when(i == n_tiles)
    def _():
        normalise_previous()


def _ffn_block(hx, hm, w_in, w_out, gain, bias, sel):
    n, d = hx.shape
    f = w_out.shape[2]
    tm = SUB_TILE
    assert n % tm == 0
    n_tiles = n // tm
    with_meta = hm is not None
    x_spec = pl.BlockSpec((tm, d), lambda i: (jnp.minimum(i, n_tiles - 1), 0))
    o_spec = pl.BlockSpec((tm, d), lambda i: (jnp.maximum(i - 1, 0), 0))
    hbm = pl.BlockSpec(memory_space=pl.ANY)
    w_specs = [hbm, hbm, _resident((1, d)), _resident((1, d))]
    params = pltpu.CompilerParams(dimension_semantics=("arbitrary",), vmem_limit_bytes=VMEM_LIMIT_BYTES)
    scratch = [pltpu.VMEM((tm, d), F32),
               pltpu.VMEM((d, 2 * f), F32), pltpu.VMEM((f, d), F32),
               pltpu.SemaphoreType.DMA((3,))]
    if not with_meta:
        return pl.pallas_call(
            lambda *refs: _ffn_kernel(False, n_tiles, sel, *refs),
            grid=(n_tiles + 1,), in_specs=[x_spec] + w_specs, out_specs=o_spec,
            out_shape=jax.ShapeDtypeStruct((n, d), F32), scratch_shapes=scratch,
            compiler_params=params, name="ffn_block",
        )(hx, w_in, w_out, gain, bias), None
    return pl.pallas_call(
        lambda *refs: _ffn_kernel(True, n_tiles, sel, *refs),
        grid=(n_tiles + 1,), in_specs=[x_spec, _resident((N_META, d))] + w_specs,
        out_specs=(o_spec, pl.BlockSpec((N_META, d), lambda i: (0, 0))),
        out_shape=(jax.ShapeDtypeStruct((n, d), F32), jax.ShapeDtypeStruct((N_META, d), F32)),
        scratch_shapes=scratch, compiler_params=params, name="ffn_meta_block",
    )(hx, hm, w_in, w_out, gain, bias)


def _conv_rows(x, prev, w_in_ref, w_conv_ref, w_out_ref, gain_ref, bias_ref):
    n, d = x.shape
    xb = x.astype(BF16)
    u = _dot(xb, w_in_ref[:, d:2 * d]) * _dot(xb, w_in_ref[:, 2 * d:])
    bgate = _dot(xb, w_in_ref[:, :d])
    row = lax.broadcasted_iota(jnp.int32, (n, 1), 0)
    last = prev[CONV_CARRY_ROWS - 1:CONV_CARRY_ROWS]
    u1 = jnp.where(row == 0, last, pltpu.roll(u, 1, 0))
    u2 = jnp.where(row == 0, prev[CONV_CARRY_ROWS - 2:CONV_CARRY_ROWS - 1],
                   jnp.where(row == 1, last, pltpu.roll(u, 2, 0)))
    w = w_conv_ref[...]
    conv = w[0:1] * u2 + w[1:2] * u1 + w[2:3] * u
    mix = _dot((bgate * conv).astype(BF16), w_out_ref[...])
    out = _layer_norm(DEEPNORM_ALPHA * x + mix, gain_ref[...], bias_ref[...])
    return out, u[n - CONV_CARRY_ROWS:]


def _conv_kernel(x_ref, m_ref, w_in_ref, w_conv_ref, w_out_ref, gain_ref, bias_ref,
                 ox_ref, om_ref, carry_ref, meta_carry_ref):
    weights = (w_in_ref, w_conv_ref, w_out_ref, gain_ref, bias_ref)

    @pl.when((pl.program_id(0) == 0) & (pl.program_id(1) == 0))
    def _():
        zeros = jnp.zeros((CONV_CARRY_ROWS, x_ref.shape[2]), F32)
        om_ref[...], meta_carry_ref[...] = _conv_rows(m_ref[...], zeros, *weights)

    @pl.when(pl.program_id(1) == 0)
    def _():
        carry_ref[...] = meta_carry_ref[...]

    prev = carry_ref[...]
    for s in range(x_ref.shape[1] // SUB_TILE):
        rows = slice(s * SUB_TILE, (s + 1) * SUB_TILE)
        ox_ref[0, rows, :], prev = _conv_rows(x_ref[0, rows, :], prev, *weights)
    carry_ref[...] = prev


def _conv_block(hx, hm, w_in, w_conv, w_out, gain, bias):
    bsz, seq, d = hx.shape
    tm = SUB_TILE * CONV_SUBTILES
    assert seq % tm == 0 and CONV_WIDTH - 1 <= CONV_CARRY_ROWS <= N_META
    x_spec = pl.BlockSpec((1, tm, d), lambda b, t: (b, t, 0))
    return pl.pallas_call(
        _conv_kernel,
        grid=(bsz, seq // tm),
        in_specs=[x_spec, _resident((N_META, d)), _resident((d, 3 * d)), _resident((CONV_WIDTH, d)),
                  _resident((d, d)), _resident((1, d)), _resident((1, d))],
        out_specs=(x_spec, pl.BlockSpec((N_META, d), lambda b, t: (0, 0))),
        out_shape=(jax.ShapeDtypeStruct((bsz, seq, d), F32), jax.ShapeDtypeStruct((N_META, d), F32)),
        scratch_shapes=[pltpu.VMEM((CONV_CARRY_ROWS, d), F32), pltpu.VMEM((CONV_CARRY_ROWS, d), F32)],
        compiler_params=pltpu.CompilerParams(
            dimension_semantics=("arbitrary", "arbitrary"), vmem_limit_bytes=VMEM_LIMIT_BYTES),
        name="conv_block",
    )(hx, hm, w_in, w_conv, w_out, gain, bias)


def _split3_bf16(a):
    hi = a.astype(BF16)
    r = a - hi.astype(F32)
    mid = r.astype(BF16)
    lo = (r - mid.astype(F32)).astype(BF16)
    return hi, mid, lo


def _gla_rows(x, c, state_ref, weights, scratch):
    (w_in_ref, w_gu_ref, b_gate_ref, norm_w_ref, w_out_ref, gain_ref, bias_ref) = weights
    (q_ref, k_ref, v_ref, la_ref, b_ref, att_ref, qin_ref, qt_ref, kt_ref, kst_ref, kstT_ref,
     p_ref, upd_ref, gate_ref) = scratch
    n = x.shape[0]
    n_chunks = n // c
    all_rows = slice(0, n)
    o_k, o_v, o_g, o_gz = GLA_DK, 2 * GLA_DK, 2 * GLA_DK + GLA_DV, 2 * GLA_DK + 2 * GLA_DV
    xb = x.astype(BF16)
    gz = _dot(xb, w_in_ref[:, o_gz:])
    q_ref[all_rows, :] = _dot(xb, w_in_ref[:, :o_k]) * (GLA_HEAD_K ** -0.5)
    z = _dot(gz.astype(BF16), w_gu_ref[...]) + b_gate_ref[...]
    k_ref[all_rows, :] = _dot(xb, w_in_ref[:, o_k:o_v])
    v_ref[all_rows, :] = _dot(xb, w_in_ref[:, o_v:o_g]).astype(BF16)
    la_ref[all_rows, :] = (jnp.minimum(z, 0.0) - jnp.log1p(jnp.exp(-jnp.abs(z)))) * (1.0 / GLA_GATE_NORMALIZER)

    ri = lax.broadcasted_iota(jnp.int32, (c, c), 0)
    ci = lax.broadcasted_iota(jnp.int32, (c, c), 1)
    causal = ci <= ri
    tril = causal.astype(BF16)
    head_k = [slice(hd * GLA_HEAD_K, (hd + 1) * GLA_HEAD_K) for hd in range(GLA_HEADS)]
    head_v = [slice(hd * GLA_HEAD_V, (hd + 1) * GLA_HEAD_V) for hd in range(GLA_HEADS)]
    chunk_rows = [slice(ic * c, (ic + 1) * c) for ic in range(n_chunks)]

    for rows in chunk_rows:
        hi, mid, lo = _split3_bf16(la_ref[rows, :])
        b_ref[rows, :] = _dot(tril, hi) + _dot(tril, mid) + _dot(tril, lo)
    gate = _dot(xb, w_in_ref[:, o_g:o_gz])
    gate_ref[all_rows, :] = gate * jax.nn.sigmoid(gate)
    decays = []
    for rows in chunk_rows:
        b = b_ref[rows, :]
        b_mid = b[c // 2:c // 2 + 1]
        b_last = b[c - 1:c]
        q_t = q_ref[rows, :] * jnp.exp(b - b_mid)
        k_t = k_ref[rows, :] * jnp.exp(b_mid - b)
        qin_ref[rows, :] = (q_t * jnp.exp(b_mid)).astype(BF16)
        kst_ref[rows, :] = (k_t * jnp.exp(b_last - b_mid)).astype(BF16)
        qt_ref[rows, :] = q_t.astype(BF16)
        kt_ref[rows, :] = k_t.astype(BF16)
        decays.append(jnp.exp(b_last))
    for ic, rows in enumerate(chunk_rows):
        kstT_ref[ic, :, 0:c] = jnp.transpose(kst_ref[rows, :])
    for ic, rows in enumerate(chunk_rows):
        for hd in range(GLA_HEADS):
            scores = lax.dot_general(qt_ref[rows, head_k[hd]], kt_ref[rows, head_k[hd]],
                                     (((1,), (1,)), ((), ())), preferred_element_type=F32)
            p_ref[hd, rows, 0:c] = jnp.where(causal, scores, 0.0).astype(BF16)
            upd_ref[ic, hd] = _dot(kstT_ref[ic, head_k[hd], 0:c], v_ref[rows, head_v[hd]])

    pad = [jnp.zeros_like(decays[0])] * (-n_chunks % 8)
    dec_rows = jnp.concatenate(decays + pad, axis=0)
    dec_cols = [jnp.transpose(dec_rows[:, head_k[hd]]) for hd in range(GLA_HEADS)]

    for ic, rows in enumerate(chunk_rows):
        for hd in range(GLA_HEADS):
            s_prev = state_ref[hd]
            att_ref[rows, head_v[hd]] = (_dot(p_ref[hd, rows, 0:c], v_ref[rows, head_v[hd]])
                                         + _dot(qin_ref[rows, head_k[hd]], s_prev.astype(BF16)))
            state_ref[hd] = dec_cols[hd][:, ic:ic + 1] * s_prev + upd_ref[ic, hd]

    norm_w = norm_w_ref[...]
    heads = []
    for hd in range(GLA_HEADS):
        o = att_ref[all_rows, head_v[hd]]
        o = o * lax.rsqrt(jnp.mean(o * o, axis=-1, keepdims=True) + RMS_EPS) * norm_w
        heads.append((o * gate_ref[all_rows, head_v[hd]]).astype(BF16))
    mix = _dot(jnp.concatenate(heads, axis=1), w_out_ref[...])
    return _layer_norm(DEEPNORM_ALPHA * x + mix, gain_ref[...], bias_ref[...])


def _gla_kernel(x_ref, m_ref, w_in_ref, w_gu_ref, b_gate_ref, norm_w_ref, w_out_ref, gain_ref, bias_ref,
                ox_ref, om_ref, state_ref, meta_state_ref, *scratch):
    weights = (w_in_ref, w_gu_ref, b_gate_ref, norm_w_ref, w_out_ref, gain_ref, bias_ref)

    @pl.when((pl.program_id(0) == 0) & (pl.program_id(1) == 0))
    def _():
        meta_state_ref[...] = jnp.zeros_like(meta_state_ref)
        om_ref[...] = _gla_rows(m_ref[...], N_META, meta_state_ref, weights, scratch)

    @pl.when(pl.program_id(1) == 0)
    def _():
        state_ref[...] = meta_state_ref[...]

    ox_ref[0] = _gla_rows(x_ref[0], GLA_CHUNK, state_ref, weights, scratch)


def _gla_block(hx, hm, w_in, w_gu, b_gate, norm_w, w_out, gain, bias):
    bsz, seq, d = hx.shape
    tm = GLA_TILE
    assert seq % tm == 0 and tm % GLA_CHUNK == 0 and N_META <= GLA_CHUNK
    n_chunks = tm // GLA_CHUNK
    x_spec = pl.BlockSpec((1, tm, d), lambda b, t: (b, t, 0))
    state = pltpu.VMEM((GLA_HEADS, GLA_HEAD_K, GLA_HEAD_V), F32)
    return pl.pallas_call(
        _gla_kernel,
        grid=(bsz, seq // tm),
        in_specs=[x_spec, _resident((N_META, d)), _resident(w_in.shape), _resident((GLA_GATE_RANK, GLA_DK)),
                  _resident((1, GLA_DK)), _resident((1, GLA_HEAD_V)), _resident((GLA_DV, d)),
                  _resident((1, d)), _resident((1, d))],
        out_specs=(x_spec, pl.BlockSpec((N_META, d), lambda b, t: (0, 0))),
        out_shape=(jax.ShapeDtypeStruct((bsz, seq, d), F32), jax.ShapeDtypeStruct((N_META, d), F32)),
        scratch_shapes=[
            state, state,
            pltpu.VMEM((tm, GLA_DK), F32),
            pltpu.VMEM((tm, GLA_DK), F32),
            pltpu.VMEM((tm, GLA_DV), BF16),
            pltpu.VMEM((tm, GLA_DK), F32),
            pltpu.VMEM((tm, GLA_DK), F32),
            pltpu.VMEM((tm, GLA_DV), F32),
            pltpu.VMEM((tm, GLA_DK), BF16),
            pltpu.VMEM((tm, GLA_DK), BF16),
            pltpu.VMEM((tm, GLA_DK), BF16),
            pltpu.VMEM((tm, GLA_DK), BF16),
            pltpu.VMEM((n_chunks, GLA_DK, GLA_CHUNK), BF16),
            pltpu.VMEM((GLA_HEADS, tm, GLA_CHUNK), BF16),
            pltpu.VMEM((n_chunks, GLA_HEADS, GLA_HEAD_K, GLA_HEAD_V), F32),
            pltpu.VMEM((tm, GLA_DV), F32),
        ],
        compiler_params=pltpu.CompilerParams(
            dimension_semantics=("arbitrary", "arbitrary"), vmem_limit_bytes=VMEM_LIMIT_BYTES),
        name="gla_block",
    )(hx, hm, w_in, w_gu, b_gate, norm_w, w_out, gain, bias)


def kernel(x, meta_tokens, ln_gain, ln_bias, ffn_w_in, ffn_w_out, conv_w_in, conv_w, conv_w_out,
           gla_w_in, gla_w_gate_up, gla_b_gate, gla_norm_w, gla_w_out):
    bsz, seq, d = x.shape
    hx = x
    hm = meta_tokens.astype(x.dtype)

    def ffn(hx, hm, i, half):
        ox, om = _ffn_block(hx.reshape(bsz * seq, d), hm, ffn_w_in, ffn_w_out,
                            ln_gain[i, 2 * half][None], ln_bias[i, 2 * half][None], (i, half))
        return ox.reshape(bsz, seq, d), om

    for i in range(DEPTH):
        hx, hm = ffn(hx, hm, i, 0)
        j = i // 2
        gain, bias = ln_gain[i, 1][None], ln_bias[i, 1][None]
        if i % 2 == 0:
            hx, hm = _conv_block(hx, hm, conv_w_in[j], conv_w[j], conv_w_out[j], gain, bias)
        else:
            hx, hm = _gla_block(hx, hm, gla_w_in[j], gla_w_gate_up[j].astype(BF16),
                                gla_b_gate[j][None], gla_norm_w[j][None], gla_w_out[j], gain, bias)
        hx, hm = ffn(hx, hm if i + 1 < DEPTH else None, i, 1)
    return hx
```

```python
import jax
import jax.numpy as jnp
from jax import lax
from jax.experimental import pallas as pl
from jax.experimental.pallas import tpu as pltpu

D_MODEL = 1024
DEPTH = 2
N_META = 16
D_FF = 2816
CONV_WIDTH = 3
GLA_HEADS = 4
GLA_DK = D_MODEL // 2
GLA_DV = D_MODEL
GLA_HEAD_K = GLA_DK // GLA_HEADS
GLA_HEAD_V = GLA_DV // GLA_HEADS
GLA_GATE_RANK = 16
GLA_GATE_NORMALIZER = 16.0
GLA_CHUNK = 64
DEEPNORM_ALPHA = (2.0 * DEPTH) ** 0.25
LN_EPS = 1e-5
RMS_EPS = 1e-6

CONV_CARRY_ROWS = 8
FFN_TILE = 512
CONV_SUB_TILE = 256
CONV_SUBTILES = 4
GLA_TILE = 512
GLA_OUT_MIN_ROWS = 256
VMEM_LIMIT_BYTES = 56 * 1024 * 1024

F32 = jnp.float32
BF16 = jnp.bfloat16


def _dot(a, b):
    return jnp.dot(a, b, preferred_element_type=F32)


def _layer_norm(y, gain, bias):
    mu = jnp.mean(y, axis=-1, keepdims=True)
    yc = y - mu
    var = jnp.mean(yc * yc, axis=-1, keepdims=True)
    return yc * lax.rsqrt(var + LN_EPS) * gain + bias


def _resident(shape):
    zeros = (0,) * len(shape)
    return pl.BlockSpec(shape, lambda *_: zeros, pipeline_mode=pl.Buffered(1))


def _ffn_residual(x, w_in_ref, w_out_ref, before_gate=None, before_up=None, before_out=None):
    f = w_out_ref.shape[0]
    xb = x.astype(BF16)
    if before_gate is not None:
        before_gate()
    gate = _dot(xb, w_in_ref[:, :f])
    if before_up is not None:
        before_up()
    up = _dot(xb, w_in_ref[:, f:])
    act = (gate * jax.nn.sigmoid(gate) * up).astype(BF16)
    if before_out is not None:
        before_out()
    return DEEPNORM_ALPHA * x + 0.5 * _dot(act, w_out_ref[...])


def _ffn_kernel(with_meta, n_tiles, sel, *refs):
    if with_meta:
        (x_ref, m_ref, w_in_hbm, w_out_hbm, gain_ref, bias_ref, ox_ref, om_ref,
         y_ref, w_in_ref, w_out_ref, sem) = refs
    else:
        x_ref, w_in_hbm, w_out_hbm, gain_ref, bias_ref, ox_ref, y_ref, w_in_ref, w_out_ref, sem = refs
    i = pl.program_id(0)
    f = w_out_ref.shape[0]
    layer, half = sel
    copies = (
        pltpu.make_async_copy(w_in_hbm.at[layer, half, :, pl.ds(0, f)], w_in_ref.at[:, pl.ds(0, f)], sem.at[0]),
        pltpu.make_async_copy(w_in_hbm.at[layer, half, :, pl.ds(f, f)], w_in_ref.at[:, pl.ds(f, f)], sem.at[1]),
        pltpu.make_async_copy(w_out_hbm.at[layer, half], w_out_ref, sem.at[2]),
    )

    def normalise_previous():
        ox_ref[...] = _layer_norm(y_ref[...], gain_ref[...], bias_ref[...])

    def form_current(*hooks):
        y_ref[...] = _ffn_residual(x_ref[...], w_in_ref, w_out_ref, *hooks)

    @pl.when(i == 0)
    def _():
        for copy in copies:
            copy.start()
        form_current(*(copy.wait for copy in copies))
        if with_meta:
            om_ref[...] = _layer_norm(_ffn_residual(m_ref[...], w_in_ref, w_out_ref),
                                      gain_ref[...], bias_ref[...])

    @pl.when((i > 0) & (i < n_tiles))
    def _():
        normalise_previous()
        form_current()

    @pl.when(i == n_tiles)
    def _():
        normalise_previous()


def _ffn_block(hx, hm, w_in, w_out, gain, bias, sel):
    n, d = hx.shape
    f = w_out.shape[2]
    tm = FFN_TILE
    assert n % tm == 0
    n_tiles = n // tm
    with_meta = hm is not None
    x_spec = pl.BlockSpec((tm, d), lambda i: (jnp.minimum(i, n_tiles - 1), 0))
    o_spec = pl.BlockSpec((tm, d), lambda i: (jnp.maximum(i - 1, 0), 0))
    hbm = pl.BlockSpec(memory_space=pl.ANY)
    w_specs = [hbm, hbm, _resident((1, d)), _resident((1, d))]
    params = pltpu.CompilerParams(dimension_semantics=("arbitrary",), vmem_limit_bytes=VMEM_LIMIT_BYTES)
    scratch = [pltpu.VMEM((tm, d), F32),
               pltpu.VMEM((d, 2 * f), F32), pltpu.VMEM((f, d), F32),
               pltpu.SemaphoreType.DMA((3,))]
    if not with_meta:
        return pl.pallas_call(
            lambda *refs: _ffn_kernel(False, n_tiles, sel, *refs),
            grid=(n_tiles + 1,), in_specs=[x_spec] + w_specs, out_specs=o_spec,
            out_shape=jax.ShapeDtypeStruct((n, d), F32), scratch_shapes=scratch,
            compiler_params=params, name="ffn_block",
        )(hx, w_in, w_out, gain, bias), None
    return pl.pallas_call(
        lambda *refs: _ffn_kernel(True, n_tiles, sel, *refs),
        grid=(n_tiles + 1,), in_specs=[x_spec, _resident((N_META, d))] + w_specs,
        out_specs=(o_spec, pl.BlockSpec((N_META, d), lambda i: (0, 0))),
        out_shape=(jax.ShapeDtypeStruct((n, d), F32), jax.ShapeDtypeStruct((N_META, d), F32)),
        scratch_shapes=scratch, compiler_params=params, name="ffn_meta_block",
    )(hx, hm, w_in, w_out, gain, bias)


def _conv_rows(x, prev, w_in_ref, w_conv_ref, w_out_ref, gain_ref, bias_ref):
    n, d = x.shape
    xb = x.astype(BF16)
    u = _dot(xb, w_in_ref[:, d:2 * d]) * _dot(xb, w_in_ref[:, 2 * d:])
    bgate = _dot(xb, w_in_ref[:, :d])
    row = lax.broadcasted_iota(jnp.int32, (n, 1), 0)
    last = prev[CONV_CARRY_ROWS - 1:CONV_CARRY_ROWS]
    u1 = jnp.where(row == 0, last, pltpu.roll(u, 1, 0))
    u2 = jnp.where(row == 0, prev[CONV_CARRY_ROWS - 2:CONV_CARRY_ROWS - 1],
                   jnp.where(row == 1, last, pltpu.roll(u, 2, 0)))
    w = w_conv_ref[...]
    conv = w[0:1] * u2 + w[1:2] * u1 + w[2:3] * u
    mix = _dot((bgate * conv).astype(BF16), w_out_ref[...])
    out = _layer_norm(DEEPNORM_ALPHA * x + mix, gain_ref[...], bias_ref[...])
    return out, u[n - CONV_CARRY_ROWS:]


def _conv_kernel(x_ref, m_ref, w_in_ref, w_conv_ref, w_out_ref, gain_ref, bias_ref,
                 ox_ref, om_ref, carry_ref, meta_carry_ref):
    weights = (w_in_ref, w_conv_ref, w_out_ref, gain_ref, bias_ref)

    @pl.when((pl.program_id(0) == 0) & (pl.program_id(1) == 0))
    def _():
        zeros = jnp.zeros((CONV_CARRY_ROWS, x_ref.shape[2]), F32)
        om_ref[...], meta_carry_ref[...] = _conv_rows(m_ref[...], zeros, *weights)

    @pl.when(pl.program_id(1) == 0)
    def _():
        carry_ref[...] = meta_carry_ref[...]

    prev = carry_ref[...]
    for s in range(CONV_SUBTILES):
        rows = slice(s * CONV_SUB_TILE, (s + 1) * CONV_SUB_TILE)
        ox_ref[0, rows, :], prev = _conv_rows(x_ref[0, rows, :], prev, *weights)
    carry_ref[...] = prev


def _conv_block(hx, hm, w_in, w_conv, w_out, gain, bias):
    bsz, seq, d = hx.shape
    tm = CONV_SUB_TILE * CONV_SUBTILES
    assert seq % tm == 0 and CONV_WIDTH - 1 <= CONV_CARRY_ROWS <= N_META
    x_spec = pl.BlockSpec((1, tm, d), lambda b, t: (b, t, 0))
    return pl.pallas_call(
        _conv_kernel,
        grid=(bsz, seq // tm),
        in_specs=[x_spec, _resident((N_META, d)), _resident((d, 3 * d)), _resident((CONV_WIDTH, d)),
                  _resident((d, d)), _resident((1, d)), _resident((1, d))],
        out_specs=(x_spec, pl.BlockSpec((N_META, d), lambda b, t: (0, 0))),
        out_shape=(jax.ShapeDtypeStruct((bsz, seq, d), F32), jax.ShapeDtypeStruct((N_META, d), F32)),
        scratch_shapes=[pltpu.VMEM((CONV_CARRY_ROWS, d), F32), pltpu.VMEM((CONV_CARRY_ROWS, d), F32)],
        compiler_params=pltpu.CompilerParams(
            dimension_semantics=("arbitrary", "arbitrary"), vmem_limit_bytes=VMEM_LIMIT_BYTES),
        name="conv_block",
    )(hx, hm, w_in, w_conv, w_out, gain, bias)


def _split3_bf16(a):
    hi = a.astype(BF16)
    r = a - hi.astype(F32)
    mid = r.astype(BF16)
    lo = (r - mid.astype(F32)).astype(BF16)
    return hi, mid, lo


def _gla_rows(x, c, state_ref, weights, scratch):
    (w_in_ref, w_gu_ref, b_gate_ref, norm_w_ref, w_out_ref, gain_ref, bias_ref) = weights
    (q_ref, k_ref, v_ref, la_ref, b_ref, att_ref, qin_ref, qt_ref, kt_ref, kst_ref, kstT_ref,
     p_ref, upd_ref, gate_ref) = scratch
    n = x.shape[0]
    n_chunks = n // c
    all_rows = slice(0, n)
    o_k, o_v, o_g, o_gz = GLA_DK, 2 * GLA_DK, 2 * GLA_DK + GLA_DV, 2 * GLA_DK + 2 * GLA_DV
    xb = x.astype(BF16)
    gz = _dot(xb, w_in_ref[:, o_gz:])
    q_ref[all_rows, :] = _dot(xb, w_in_ref[:, :o_k]) * (GLA_HEAD_K ** -0.5)
    z = _dot(gz.astype(BF16), w_gu_ref[...]) + b_gate_ref[...]
    k_ref[all_rows, :] = _dot(xb, w_in_ref[:, o_k:o_v])
    v_ref[all_rows, :] = _dot(xb, w_in_ref[:, o_v:o_g]).astype(BF16)
    la_ref[all_rows, :] = (jnp.minimum(z, 0.0) - jnp.log1p(jnp.exp(-jnp.abs(z)))) * (1.0 / GLA_GATE_NORMALIZER)

    ri = lax.broadcasted_iota(jnp.int32, (c, c), 0)
    ci = lax.broadcasted_iota(jnp.int32, (c, c), 1)
    causal = ci <= ri
    tril = causal.astype(BF16)
    head_k = [slice(hd * GLA_HEAD_K, (hd + 1) * GLA_HEAD_K) for hd in range(GLA_HEADS)]
    head_v = [slice(hd * GLA_HEAD_V, (hd + 1) * GLA_HEAD_V) for hd in range(GLA_HEADS)]
    chunk_rows = [slice(ic * c, (ic + 1) * c) for ic in range(n_chunks)]

    for rows in chunk_rows:
        hi, mid, lo = _split3_bf16(la_ref[rows, :])
        b_ref[rows, :] = _dot(tril, hi) + _dot(tril, mid) + _dot(tril, lo)
    gate = _dot(xb, w_in_ref[:, o_g:o_gz])
    gate_ref[all_rows, :] = gate * jax.nn.sigmoid(gate)
    decays = []
    for rows in chunk_rows:
        b = b_ref[rows, :]
        b_mid = b[c // 2:c // 2 + 1]
        b_last = b[c - 1:c]
        q_t = q_ref[rows, :] * jnp.exp(b - b_mid)
        k_t = k_ref[rows, :] * jnp.exp(b_mid - b)
        qin_ref[rows, :] = (q_t * jnp.exp(b_mid)).astype(BF16)
        kst_ref[rows, :] = (k_t * jnp.exp(b_last - b_mid)).astype(BF16)
        qt_ref[rows, :] = q_t.astype(BF16)
        kt_ref[rows, :] = k_t.astype(BF16)
        decays.append(jnp.exp(b_last))
    for ic, rows in enumerate(chunk_rows):
        kstT_ref[ic, :, 0:c] = jnp.transpose(kst_ref[rows, :])
    for ic, rows in enumerate(chunk_rows):
        for hd in range(GLA_HEADS):
            scores = lax.dot_general(qt_ref[rows, head_k[hd]], kt_ref[rows, head_k[hd]],
                                     (((1,), (1,)), ((), ())), preferred_element_type=F32)
            p_ref[hd, rows, 0:c] = jnp.where(causal, scores, 0.0).astype(BF16)
            upd_ref[ic, hd] = _dot(kstT_ref[ic, head_k[hd], 0:c], v_ref[rows, head_v[hd]])

    pad = [jnp.zeros_like(decays[0])] * (-n_chunks % 8)
    dec_rows = jnp.concatenate(decays + pad, axis=0)
    dec_cols = [jnp.transpose(dec_rows[:, head_k[hd]]) for hd in range(GLA_HEADS)]

    for ic, rows in enumerate(chunk_rows):
        for hd in range(GLA_HEADS):
            s_prev = state_ref[hd]
            att_ref[rows, head_v[hd]] = (_dot(p_ref[hd, rows, 0:c], v_ref[rows, head_v[hd]])
                                         + _dot(qin_ref[rows, head_k[hd]], s_prev.astype(BF16)))
            state_ref[hd] = dec_cols[hd][:, ic:ic + 1] * s_prev + upd_ref[ic, hd]

    norm_w = norm_w_ref[...]
    heads = []
    for hd in range(GLA_HEADS):
        o = att_ref[all_rows, head_v[hd]]
        o = o * lax.rsqrt(jnp.mean(o * o, axis=-1, keepdims=True) + RMS_EPS) * norm_w
        heads.append((o * gate_ref[all_rows, head_v[hd]]).astype(BF16))
    gated = jnp.concatenate(heads, axis=1)
    halves = (slice(0, n // 2), slice(n // 2, n)) if n >= 2 * GLA_OUT_MIN_ROWS else (all_rows,)
    outs = []
    for r in halves:
        mix = _dot(gated[r], w_out_ref[...])
        outs.append(_layer_norm(DEEPNORM_ALPHA * x[r] + mix, gain_ref[...], bias_ref[...]))
    return jnp.concatenate(outs, axis=0)


def _gla_kernel(x_ref, m_ref, w_in_ref, w_gu_ref, b_gate_ref, norm_w_ref, w_out_ref, gain_ref, bias_ref,
                ox_ref, om_ref, state_ref, meta_state_ref, *scratch):
    weights = (w_in_ref, w_gu_ref, b_gate_ref, norm_w_ref, w_out_ref, gain_ref, bias_ref)

    @pl.when((pl.program_id(0) == 0) & (pl.program_id(1) == 0))
    def _():
        meta_state_ref[...] = jnp.zeros_like(meta_state_ref)
        om_ref[...] = _gla_rows(m_ref[...], N_META, meta_state_ref, weights, scratch)

    @pl.when(pl.program_id(1) == 0)
    def _():
        state_ref[...] = meta_state_ref[...]

    ox_ref[0] = _gla_rows(x_ref[0], GLA_CHUNK, state_ref, weights, scratch)


def _gla_block(hx, hm, w_in, w_gu, b_gate, norm_w, w_out, gain, bias):
    bsz, seq, d = hx.shape
    tm = GLA_TILE
    assert seq % tm == 0 and tm % GLA_CHUNK == 0 and N_META <= GLA_CHUNK
    n_chunks = tm // GLA_CHUNK
    x_spec = pl.BlockSpec((1, tm, d), lambda b, t: (b, t, 0))
    state = pltpu.VMEM((GLA_HEADS, GLA_HEAD_K, GLA_HEAD_V), F32)
    return pl.pallas_call(
        _gla_kernel,
        grid=(bsz, seq // tm),
        in_specs=[x_spec, _resident((N_META, d)), _resident(w_in.shape), _resident((GLA_GATE_RANK, GLA_DK)),
                  _resident((1, GLA_DK)), _resident((1, GLA_HEAD_V)), _resident((GLA_DV, d)),
                  _resident((1, d)), _resident((1, d))],
        out_specs=(x_spec, pl.BlockSpec((N_META, d), lambda b, t: (0, 0))),
        out_shape=(jax.ShapeDtypeStruct((bsz, seq, d), F32), jax.ShapeDtypeStruct((N_META, d), F32)),
        scratch_shapes=[
            state, state,
            pltpu.VMEM((tm, GLA_DK), F32),
            pltpu.VMEM((tm, GLA_DK), F32),
            pltpu.VMEM((tm, GLA_DV), BF16),
            pltpu.VMEM((tm, GLA_DK), F32),
            pltpu.VMEM((tm, GLA_DK), F32),
            pltpu.VMEM((tm, GLA_DV), F32),
            pltpu.VMEM((tm, GLA_DK), BF16),
            pltpu.VMEM((tm, GLA_DK), BF16),
            pltpu.VMEM((tm, GLA_DK), BF16),
            pltpu.VMEM((tm, GLA_DK), BF16),
            pltpu.VMEM((n_chunks, GLA_DK, GLA_CHUNK), BF16),
            pltpu.VMEM((GLA_HEADS, tm, GLA_CHUNK), BF16),
            pltpu.VMEM((n_chunks, GLA_HEADS, GLA_HEAD_K, GLA_HEAD_V), F32),
            pltpu.VMEM((tm, GLA_DV), F32),
        ],
        compiler_params=pltpu.CompilerParams(
            dimension_semantics=("arbitrary", "arbitrary"), vmem_limit_bytes=VMEM_LIMIT_BYTES),
        name="gla_block",
    )(hx, hm, w_in, w_gu, b_gate, norm_w, w_out, gain, bias)


def kernel(x, meta_tokens, ln_gain, ln_bias, ffn_w_in, ffn_w_out, conv_w_in, conv_w, conv_w_out,
           gla_w_in, gla_w_gate_up, gla_b_gate, gla_norm_w, gla_w_out):
    bsz, seq, d = x.shape
    hx = x
    hm = meta_tokens.astype(x.dtype)

    def ffn(hx, hm, i, half):
        ox, om = _ffn_block(hx.reshape(bsz * seq, d), hm, ffn_w_in, ffn_w_out,
                            ln_gain[i, 2 * half][None], ln_bias[i, 2 * half][None], (i, half))
        return ox.reshape(bsz, seq, d), om

    for i in range(DEPTH):
        hx, hm = ffn(hx, hm, i, 0)
        j = i // 2
        gain, bias = ln_gain[i, 1][None], ln_bias[i, 1][None]
        if i % 2 == 0:
            hx, hm = _conv_block(hx, hm, conv_w_in[j], conv_w[j], conv_w_out[j], gain, bias)
        else:
            hx, hm = _gla_block(hx, hm, gla_w_in[j], gla_w_gate_up[j].astype(BF16),
                                gla_b_gate[j][None], gla_norm_w[j][None], gla_w_out[j], gain, bias)
        hx, hm = ffn(hx, hm if i + 1 < DEPTH else None, i, 1)
    return hx
```

```python
import jax
import jax.numpy as jnp
from jax import lax
from jax.experimental import pallas as pl
from jax.experimental.pallas import tpu as pltpu

D_MODEL = 1024
DEPTH = 2
N_META = 16
D_FF = 2816
CONV_WIDTH = 3
GLA_HEADS = 4
GLA_DK = D_MODEL // 2
GLA_DV = D_MODEL
GLA_HEAD_K = GLA_DK // GLA_HEADS
GLA_HEAD_V = GLA_DV // GLA_HEADS
GLA_GATE_RANK = 16
GLA_GATE_NORMALIZER = 16.0
GLA_CHUNK = 64
DEEPNORM_ALPHA = (2.0 * DEPTH) ** 0.25
LN_EPS = 1e-5
RMS_EPS = 1e-6

CONV_CARRY_ROWS = 8
FFN_TILE = 512
CONV_SUB_TILE = 256
CONV_SUBTILES = 4
GLA_TILE = 512
GLA_OUT_MIN_ROWS = 256
VMEM_LIMIT_BYTES = 56 * 1024 * 1024

F32 = jnp.float32
BF16 = jnp.bfloat16


def _dot(a, b):
    return jnp.dot(a, b, preferred_element_type=F32)


def _layer_norm(y, gain, bias):
    mu = jnp.mean(y, axis=-1, keepdims=True)
    yc = y - mu
    var = jnp.mean(yc * yc, axis=-1, keepdims=True)
    return yc * lax.rsqrt(var + LN_EPS) * gain + bias


def _resident(shape):
    zeros = (0,) * len(shape)
    return pl.BlockSpec(shape, lambda *_: zeros, pipeline_mode=pl.Buffered(1))


def _ffn_residual(x, w_in_ref, w_out_ref, before_gate=None, before_up=None, before_out=None):
    f = w_out_ref.shape[0]
    xb = x.astype(BF16)
    if before_gate is not None:
        before_gate()
    gate = _dot(xb, w_in_ref[:, :f])
    if before_up is not None:
        before_up()
    up = _dot(xb, w_in_ref[:, f:])
    act = (gate * jax.nn.sigmoid(gate) * up).astype(BF16)
    if before_out is not None:
        before_out()
    return DEEPNORM_ALPHA * x + 0.5 * _dot(act, w_out_ref[...])


def _ffn_kernel(with_meta, n_tiles, sel, *refs):
    if with_meta:
        (x_ref, m_ref, w_in_hbm, w_out_hbm, gain_ref, bias_ref, ox_ref, om_ref,
         y_ref, w_in_ref, w_out_ref, sem) = refs
    else:
        x_ref, w_in_hbm, w_out_hbm, gain_ref, bias_ref, ox_ref, y_ref, w_in_ref, w_out_ref, sem = refs
    i = pl.program_id(0)
    f = w_out_ref.shape[0]
    layer, half = sel
    copies = (
        pltpu.make_async_copy(w_in_hbm.at[layer, half, :, pl.ds(0, f)], w_in_ref.at[:, pl.ds(0, f)], sem.at[0]),
        pltpu.make_async_copy(w_in_hbm.at[layer, half, :, pl.ds(f, f)], w_in_ref.at[:, pl.ds(f, f)], sem.at[1]),
        pltpu.make_async_copy(w_out_hbm.at[layer, half], w_out_ref, sem.at[2]),
    )

    def normalise_previous():
        ox_ref[...] = _layer_norm(y_ref[...], gain_ref[...], bias_ref[...])

    def form_current(*hooks):
        y_ref[...] = _ffn_residual(x_ref[...], w_in_ref, w_out_ref, *hooks)

    @pl.when(i == 0)
    def _():
        for copy in copies:
            copy.start()
        form_current(*(copy.wait for copy in copies))
        if with_meta:
            om_ref[...] = _layer_norm(_ffn_residual(m_ref[...], w_in_ref, w_out_ref),
                                      gain_ref[...], bias_ref[...])

    @pl.when((i > 0) & (i < n_tiles))
    def _():
        normalise_previous()
        form_current()

    @pl.when(i == n_tiles)
    def _():
        normalise_previous()


def _ffn_block(hx, hm, w_in, w_out, gain, bias, sel):
    n, d = hx.shape
    f = w_out.shape[2]
    tm = FFN_TILE
    assert n % tm == 0
    n_tiles = n // tm
    with_meta = hm is not None
    x_spec = pl.BlockSpec((tm, d), lambda i: (jnp.minimum(i, n_tiles - 1), 0))
    o_spec = pl.BlockSpec((tm, d), lambda i: (jnp.maximum(i - 1, 0), 0))
    hbm = pl.BlockSpec(memory_space=pl.ANY)
    w_specs = [hbm, hbm, _resident((1, d)), _resident((1, d))]
    params = pltpu.CompilerParams(dimension_semantics=("arbitrary",), vmem_limit_bytes=VMEM_LIMIT_BYTES)
    scratch = [pltpu.VMEM((tm, d), F32),
               pltpu.VMEM((d, 2 * f), F32), pltpu.VMEM((f, d), F32),
               pltpu.SemaphoreType.DMA((3,))]
    if not with_meta:
        return pl.pallas_call(
            lambda *refs: _ffn_kernel(False, n_tiles, sel, *refs),
            grid=(n_tiles + 1,), in_specs=[x_spec] + w_specs, out_specs=o_spec,
            out_shape=jax.ShapeDtypeStruct((n, d), F32), scratch_shapes=scratch,
            compiler_params=params, name="ffn_block",
        )(hx, w_in, w_out, gain, bias), None
    return pl.pallas_call(
        lambda *refs: _ffn_kernel(True, n_tiles, sel, *refs),
        grid=(n_tiles + 1,), in_specs=[x_spec, _resident((N_META, d))] + w_specs,
        out_specs=(o_spec, pl.BlockSpec((N_META, d), lambda i: (0, 0))),
        out_shape=(jax.ShapeDtypeStruct((n, d), F32), jax.ShapeDtypeStruct((N_META, d), F32)),
        scratch_shapes=scratch, compiler_params=params, name="ffn_meta_block",
    )(hx, hm, w_in, w_out, gain, bias)


def _conv_rows(x, prev, w_in_ref, w_conv_ref, w_out_ref, gain_ref, bias_ref):
    n, d = x.shape
    xb = x.astype(BF16)
    u = _dot(xb, w_in_ref[:, d:2 * d]) * _dot(xb, w_in_ref[:, 2 * d:])
    bgate = _dot(xb, w_in_ref[:, :d])
    row = lax.broadcasted_iota(jnp.int32, (n, 1), 0)
    last = prev[CONV_CARRY_ROWS - 1:CONV_CARRY_ROWS]
    u1 = jnp.where(row == 0, last, pltpu.roll(u, 1, 0))
    u2 = jnp.where(row == 0, prev[CONV_CARRY_ROWS - 2:CONV_CARRY_ROWS - 1],
                   jnp.where(row == 1, last, pltpu.roll(u, 2, 0)))
    w = w_conv_ref[...]
    conv = w[0:1] * u2 + w[1:2] * u1 + w[2:3] * u
    mix = _dot((bgate * conv).astype(BF16), w_out_ref[...])
    out = _layer_norm(DEEPNORM_ALPHA * x + mix, gain_ref[...], bias_ref[...])
    return out, u[n - CONV_CARRY_ROWS:]


def _conv_kernel(x_ref, m_ref, w_in_ref, w_conv_ref, w_out_ref, gain_ref, bias_ref,
                 ox_ref, om_ref, carry_ref, meta_carry_ref):
    weights = (w_in_ref, w_conv_ref, w_out_ref, gain_ref, bias_ref)

    @pl.when((pl.program_id(0) == 0) & (pl.program_id(1) == 0))
    def _():
        zeros = jnp.zeros((CONV_CARRY_ROWS, x_ref.shape[2]), F32)
        om_ref[...], meta_carry_ref[...] = _conv_rows(m_ref[...], zeros, *weights)

    @pl.when(pl.program_id(1) == 0)
    def _():
        carry_ref[...] = meta_carry_ref[...]

    prev = carry_ref[...]
    for s in range(CONV_SUBTILES):
        rows = slice(s * CONV_SUB_TILE, (s + 1) * CONV_SUB_TILE)
        ox_ref[0, rows, :], prev = _conv_rows(x_ref[0, rows, :], prev, *weights)
    carry_ref[...] = prev


def _conv_block(hx, hm, w_in, w_conv, w_out, gain, bias):
    bsz, seq, d = hx.shape
    tm = CONV_SUB_TILE * CONV_SUBTILES
    assert seq % tm == 0 and CONV_WIDTH - 1 <= CONV_CARRY_ROWS <= N_META
    x_spec = pl.BlockSpec((1, tm, d), lambda b, t: (b, t, 0))
    return pl.pallas_call(
        _conv_kernel,
        grid=(bsz, seq // tm),
        in_specs=[x_spec, _resident((N_META, d)), _resident((d, 3 * d)), _resident((CONV_WIDTH, d)),
                  _resident((d, d)), _resident((1, d)), _resident((1, d))],
        out_specs=(x_spec, pl.BlockSpec((N_META, d), lambda b, t: (0, 0))),
        out_shape=(jax.ShapeDtypeStruct((bsz, seq, d), F32), jax.ShapeDtypeStruct((N_META, d), F32)),
        scratch_shapes=[pltpu.VMEM((CONV_CARRY_ROWS, d), F32), pltpu.VMEM((CONV_CARRY_ROWS, d), F32)],
        compiler_params=pltpu.CompilerParams(
            dimension_semantics=("arbitrary", "arbitrary"), vmem_limit_bytes=VMEM_LIMIT_BYTES),
        name="conv_block",
    )(hx, hm, w_in, w_conv, w_out, gain, bias)


def _split3_bf16(a):
    hi = a.astype(BF16)
    r = a - hi.astype(F32)
    mid = r.astype(BF16)
    lo = (r - mid.astype(F32)).astype(BF16)
    return hi, mid, lo


def _gla_rows(x, c, state_ref, weights, scratch):
    (w_in_ref, w_gu_ref, b_gate_ref, norm_w_ref, w_out_ref, gain_ref, bias_ref) = weights
    (q_ref, k_ref, v_ref, la_ref, b_ref, og_ref, qin_ref, qt_ref, kt_ref, kst_ref, kstT_ref,
     p_ref, upd_ref, gate_ref) = scratch
    n = x.shape[0]
    n_chunks = n // c
    all_rows = slice(0, n)
    o_k, o_v, o_g, o_gz = GLA_DK, 2 * GLA_DK, 2 * GLA_DK + GLA_DV, 2 * GLA_DK + 2 * GLA_DV
    xb = x.astype(BF16)
    gz = _dot(xb, w_in_ref[:, o_gz:])
    q_ref[all_rows, :] = _dot(xb, w_in_ref[:, :o_k]) * (GLA_HEAD_K ** -0.5)
    z = _dot(gz.astype(BF16), w_gu_ref[...]) + b_gate_ref[...]
    k_ref[all_rows, :] = _dot(xb, w_in_ref[:, o_k:o_v])
    v_ref[all_rows, :] = _dot(xb, w_in_ref[:, o_v:o_g]).astype(BF16)
    la_ref[all_rows, :] = (jnp.minimum(z, 0.0) - jnp.log1p(jnp.exp(-jnp.abs(z)))) * (1.0 / GLA_GATE_NORMALIZER)

    ri = lax.broadcasted_iota(jnp.int32, (c, c), 0)
    ci = lax.broadcasted_iota(jnp.int32, (c, c), 1)
    causal = ci <= ri
    tril = causal.astype(BF16)
    head_k = [slice(hd * GLA_HEAD_K, (hd + 1) * GLA_HEAD_K) for hd in range(GLA_HEADS)]
    head_v = [slice(hd * GLA_HEAD_V, (hd + 1) * GLA_HEAD_V) for hd in range(GLA_HEADS)]
    chunk_rows = [slice(ic * c, (ic + 1) * c) for ic in range(n_chunks)]

    for rows in chunk_rows:
        hi, mid, lo = _split3_bf16(la_ref[rows, :])
        b_ref[rows, :] = _dot(tril, hi) + _dot(tril, mid) + _dot(tril, lo)
    gate = _dot(xb, w_in_ref[:, o_g:o_gz])
    gate_ref[all_rows, :] = gate * jax.nn.sigmoid(gate)
    decays = []
    for rows in chunk_rows:
        b = b_ref[rows, :]
        b_mid = b[c // 2:c // 2 + 1]
        b_last = b[c - 1:c]
        q_t = q_ref[rows, :] * jnp.exp(b - b_mid)
        k_t = k_ref[rows, :] * jnp.exp(b_mid - b)
        qin_ref[rows, :] = (q_t * jnp.exp(b_mid)).astype(BF16)
        kst_ref[rows, :] = (k_t * jnp.exp(b_last - b_mid)).astype(BF16)
        qt_ref[rows, :] = q_t.astype(BF16)
        kt_ref[rows, :] = k_t.astype(BF16)
        decays.append(jnp.exp(b_last))
    for ic, rows in enumerate(chunk_rows):
        kstT_ref[ic, :, 0:c] = jnp.transpose(kst_ref[rows, :])
    for ic, rows in enumerate(chunk_rows):
        for hd in range(GLA_HEADS):
            scores = lax.dot_general(qt_ref[rows, head_k[hd]], kt_ref[rows, head_k[hd]],
                                     (((1,), (1,)), ((), ())), preferred_element_type=F32)
            p_ref[hd, rows, 0:c] = jnp.where(causal, scores, 0.0).astype(BF16)
            upd_ref[ic, hd] = _dot(kstT_ref[ic, head_k[hd], 0:c], v_ref[rows, head_v[hd]])

    pad = [jnp.zeros_like(decays[0])] * (-n_chunks % 8)
    dec_rows = jnp.concatenate(decays + pad, axis=0)
    dec_cols = [jnp.transpose(dec_rows[:, head_k[hd]]) for hd in range(GLA_HEADS)]

    norm_w = norm_w_ref[...]
    for ic, rows in enumerate(chunk_rows):
        for hd in range(GLA_HEADS):
            s_prev = state_ref[hd]
            o = (_dot(p_ref[hd, rows, 0:c], v_ref[rows, head_v[hd]])
                 + _dot(qin_ref[rows, head_k[hd]], s_prev.astype(BF16)))
            state_ref[hd] = dec_cols[hd][:, ic:ic + 1] * s_prev + upd_ref[ic, hd]
            o = o * lax.rsqrt(jnp.mean(o * o, axis=-1, keepdims=True) + RMS_EPS) * norm_w
            og_ref[rows, head_v[hd]] = (o * gate_ref[rows, head_v[hd]]).astype(BF16)

    halves = (slice(0, n // 2), slice(n // 2, n)) if n >= 2 * GLA_OUT_MIN_ROWS else (all_rows,)
    outs = []
    for r in halves:
        mix = _dot(og_ref[r, :], w_out_ref[...])
        outs.append(_layer_norm(DEEPNORM_ALPHA * x[r] + mix, gain_ref[...], bias_ref[...]))
    return jnp.concatenate(outs, axis=0)


def _gla_kernel(x_ref, m_ref, w_in_ref, w_gu_ref, b_gate_ref, norm_w_ref, w_out_ref, gain_ref, bias_ref,
                ox_ref, om_ref, state_ref, meta_state_ref, *scratch):
    weights = (w_in_ref, w_gu_ref, b_gate_ref, norm_w_ref, w_out_ref, gain_ref, bias_ref)

    @pl.when((pl.program_id(0) == 0) & (pl.program_id(1) == 0))
    def _():
        meta_state_ref[...] = jnp.zeros_like(meta_state_ref)
        om_ref[...] = _gla_rows(m_ref[...], N_META, meta_state_ref, weights, scratch)

    @pl.when(pl.program_id(1) == 0)
    def _():
        state_ref[...] = meta_state_ref[...]

    ox_ref[0] = _gla_rows(x_ref[0], GLA_CHUNK, state_ref, weights, scratch)


def _gla_block(hx, hm, w_in, w_gu, b_gate, norm_w, w_out, gain, bias):
    bsz, seq, d = hx.shape
    tm = GLA_TILE
    assert seq % tm == 0 and tm % GLA_CHUNK == 0 and N_META <= GLA_CHUNK
    n_chunks = tm // GLA_CHUNK
    x_spec = pl.BlockSpec((1, tm, d), lambda b, t: (b, t, 0))
    state = pltpu.VMEM((GLA_HEADS, GLA_HEAD_K, GLA_HEAD_V), F32)
    return pl.pallas_call(
        _gla_kernel,
        grid=(bsz, seq // tm),
        in_specs=[x_spec, _resident((N_META, d)), _resident(w_in.shape), _resident((GLA_GATE_RANK, GLA_DK)),
                  _resident((1, GLA_DK)), _resident((1, GLA_HEAD_V)), _resident((GLA_DV, d)),
                  _resident((1, d)), _resident((1, d))],
        out_specs=(x_spec, pl.BlockSpec((N_META, d), lambda b, t: (0, 0))),
        out_shape=(jax.ShapeDtypeStruct((bsz, seq, d), F32), jax.ShapeDtypeStruct((N_META, d), F32)),
        scratch_shapes=[
            state, state,
            pltpu.VMEM((tm, GLA_DK), F32),
            pltpu.VMEM((tm, GLA_DK), F32),
            pltpu.VMEM((tm, GLA_DV), BF16),
            pltpu.VMEM((tm, GLA_DK), F32),
            pltpu.VMEM((tm, GLA_DK), F32),
            pltpu.VMEM((tm, GLA_DV), BF16),
            pltpu.VMEM((tm, GLA_DK), BF16),
            pltpu.VMEM((tm, GLA_DK), BF16),
            pltpu.VMEM((tm, GLA_DK), BF16),
            pltpu.VMEM((tm, GLA_DK), BF16),
            pltpu.VMEM((n_chunks, GLA_DK, GLA_CHUNK), BF16),
            pltpu.VMEM((GLA_HEADS, tm, GLA_CHUNK), BF16),
            pltpu.VMEM((n_chunks, GLA_HEADS, GLA_HEAD_K, GLA_HEAD_V), F32),
            pltpu.VMEM((tm, GLA_DV), F32),
        ],
        compiler_params=pltpu.CompilerParams(
            dimension_semantics=("arbitrary", "arbitrary"), vmem_limit_bytes=VMEM_LIMIT_BYTES),
        name="gla_block",
    )(hx, hm, w_in, w_gu, b_gate, norm_w, w_out, gain, bias)


def kernel(x, meta_tokens, ln_gain, ln_bias, ffn_w_in, ffn_w_out, conv_w_in, conv_w, conv_w_out,
           gla_w_in, gla_w_gate_up, gla_b_gate, gla_norm_w, gla_w_out):
    bsz, seq, d = x.shape
    hx = x
    hm = meta_tokens.astype(x.dtype)

    def ffn(hx, hm, i, half):
        ox, om = _ffn_block(hx.reshape(bsz * seq, d), hm, ffn_w_in, ffn_w_out,
                            ln_gain[i, 2 * half][None], ln_bias[i, 2 * half][None], (i, half))
        return ox.reshape(bsz, seq, d), om

    for i in range(DEPTH):
        hx, hm = ffn(hx, hm, i, 0)
        j = i // 2
        gain, bias = ln_gain[i, 1][None], ln_bias[i, 1][None]
        if i % 2 == 0:
            hx, hm = _conv_block(hx, hm, conv_w_in[j], conv_w[j], conv_w_out[j], gain, bias)
        else:
            hx, hm = _gla_block(hx, hm, gla_w_in[j], gla_w_gate_up[j].astype(BF16),
                                gla_b_gate[j][None], gla_norm_w[j][None], gla_w_out[j], gain, bias)
        hx, hm = ffn(hx, hm if i + 1 < DEPTH else None, i, 1)
    return hx
```

```python
import jax
import jax.numpy as jnp
from jax import lax
from jax.experimental import pallas as pl
from jax.experimental.pallas import tpu as pltpu

D_MODEL = 1024
DEPTH = 2
N_META = 16
CONV_WIDTH = 3
GLA_HEADS = 4
GLA_DK = D_MODEL // 2
GLA_DV = D_MODEL
GLA_HEAD_K = GLA_DK // GLA_HEADS
GLA_HEAD_V = GLA_DV // GLA_HEADS
GLA_GATE_RANK = 16
GLA_GATE_NORMALIZER = 16.0
GLA_CHUNK = 64
DEEPNORM_ALPHA = (2.0 * DEPTH) ** 0.25
LN_EPS = 1e-5
RMS_EPS = 1e-6

CONV_CARRY_ROWS = 8
FFN_TILE = 512
CONV_SUB_TILE = 256
CONV_SUBTILES = 8
GLA_TILE = 512
GLA_OUT_MIN_ROWS = 256
VMEM_LIMIT_BYTES = 56 * 1024 * 1024

F32 = jnp.float32
BF16 = jnp.bfloat16


def _dot(a, b):
    return jnp.dot(a, b, preferred_element_type=F32)


def _layer_norm(y, gain, bias):
    mu = jnp.mean(y, axis=-1, keepdims=True)
    yc = y - mu
    var = jnp.mean(yc * yc, axis=-1, keepdims=True)
    return yc * lax.rsqrt(var + LN_EPS) * gain + bias


def _resident(shape):
    zeros = (0,) * len(shape)
    return pl.BlockSpec(shape, lambda *_: zeros, pipeline_mode=pl.Buffered(1))


def _ffn_residual(x, w_in_ref, w_out_ref, before_gate=None, before_up=None, before_out=None):
    f = w_out_ref.shape[0]
    xb = x.astype(BF16)
    if before_gate is not None:
        before_gate()
    gate = _dot(xb, w_in_ref[:, :f])
    if before_up is not None:
        before_up()
    up = _dot(xb, w_in_ref[:, f:])
    act = (gate * jax.nn.sigmoid(gate) * up).astype(BF16)
    if before_out is not None:
        before_out()
    return DEEPNORM_ALPHA * x + 0.5 * _dot(act, w_out_ref[...])


def _ffn_kernel(with_meta, n_tiles, sel, *refs):
    if with_meta:
        (x_ref, m_ref, w_in_hbm, w_out_hbm, gain_ref, bias_ref, ox_ref, om_ref,
         y_ref, w_in_ref, w_out_ref, sem) = refs
    else:
        x_ref, w_in_hbm, w_out_hbm, gain_ref, bias_ref, ox_ref, y_ref, w_in_ref, w_out_ref, sem = refs
    i = pl.program_id(0)
    f = w_out_ref.shape[0]
    layer, half = sel
    copies = (
        pltpu.make_async_copy(w_in_hbm.at[layer, half, :, pl.ds(0, f)], w_in_ref.at[:, pl.ds(0, f)], sem.at[0]),
        pltpu.make_async_copy(w_in_hbm.at[layer, half, :, pl.ds(f, f)], w_in_ref.at[:, pl.ds(f, f)], sem.at[1]),
        pltpu.make_async_copy(w_out_hbm.at[layer, half], w_out_ref, sem.at[2]),
    )

    def normalise_previous():
        ox_ref[...] = _layer_norm(y_ref[...], gain_ref[...], bias_ref[...])

    def form_current(*hooks):
        y_ref[...] = _ffn_residual(x_ref[...], w_in_ref, w_out_ref, *hooks)

    @pl.when(i == 0)
    def _():
        for copy in copies:
            copy.start()
        form_current(*(copy.wait for copy in copies))
        if with_meta:
            om_ref[...] = _layer_norm(_ffn_residual(m_ref[...], w_in_ref, w_out_ref),
                                      gain_ref[...], bias_ref[...])

    @pl.when((i > 0) & (i < n_tiles))
    def _():
        normalise_previous()
        form_current()

    @pl.when(i == n_tiles)
    def _():
        normalise_previous()


def _ffn_block(hx, hm, w_in, w_out, gain, bias, sel):
    n, d = hx.shape
    f = w_out.shape[2]
    tm = FFN_TILE
    assert n % tm == 0
    n_tiles = n // tm
    with_meta = hm is not None
    x_spec = pl.BlockSpec((tm, d), lambda i: (jnp.minimum(i, n_tiles - 1), 0))
    o_spec = pl.BlockSpec((tm, d), lambda i: (jnp.maximum(i - 1, 0), 0))
    hbm = pl.BlockSpec(memory_space=pl.ANY)
    w_specs = [hbm, hbm, _resident((1, d)), _resident((1, d))]
    params = pltpu.CompilerParams(dimension_semantics=("arbitrary",), vmem_limit_bytes=VMEM_LIMIT_BYTES)
    scratch = [pltpu.VMEM((tm, d), F32),
               pltpu.VMEM((d, 2 * f), F32), pltpu.VMEM((f, d), F32),
               pltpu.SemaphoreType.DMA((3,))]
    if not with_meta:
        return pl.pallas_call(
            lambda *refs: _ffn_kernel(False, n_tiles, sel, *refs),
            grid=(n_tiles + 1,), in_specs=[x_spec] + w_specs, out_specs=o_spec,
            out_shape=jax.ShapeDtypeStruct((n, d), F32), scratch_shapes=scratch,
            compiler_params=params, name="ffn_block",
        )(hx, w_in, w_out, gain, bias), None
    return pl.pallas_call(
        lambda *refs: _ffn_kernel(True, n_tiles, sel, *refs),
        grid=(n_tiles + 1,), in_specs=[x_spec, _resident((N_META, d))] + w_specs,
        out_specs=(o_spec, pl.BlockSpec((N_META, d), lambda i: (0, 0))),
        out_shape=(jax.ShapeDtypeStruct((n, d), F32), jax.ShapeDtypeStruct((N_META, d), F32)),
        scratch_shapes=scratch, compiler_params=params, name="ffn_meta_block",
    )(hx, hm, w_in, w_out, gain, bias)


def _conv_rows(x, prev, w_in_ref, w_conv_ref, w_out_ref, gain_ref, bias_ref):
    n, d = x.shape
    xb = x.astype(BF16)
    u = _dot(xb, w_in_ref[:, d:2 * d]) * _dot(xb, w_in_ref[:, 2 * d:])
    bgate = _dot(xb, w_in_ref[:, :d])
    row = lax.broadcasted_iota(jnp.int32, (n, 1), 0)
    last = prev[CONV_CARRY_ROWS - 1:CONV_CARRY_ROWS]
    u1 = jnp.where(row == 0, last, pltpu.roll(u, 1, 0))
    u2 = jnp.where(row == 0, prev[CONV_CARRY_ROWS - 2:CONV_CARRY_ROWS - 1],
                   jnp.where(row == 1, last, pltpu.roll(u, 2, 0)))
    w = w_conv_ref[...]
    conv = w[0:1] * u2 + w[1:2] * u1 + w[2:3] * u
    mix = _dot((bgate * conv).astype(BF16), w_out_ref[...])
    out = _layer_norm(DEEPNORM_ALPHA * x + mix, gain_ref[...], bias_ref[...])
    return out, u[n - CONV_CARRY_ROWS:]


def _conv_kernel(x_ref, m_ref, w_in_ref, w_conv_ref, w_out_ref, gain_ref, bias_ref,
                 ox_ref, om_ref, carry_ref, meta_carry_ref):
    weights = (w_in_ref, w_conv_ref, w_out_ref, gain_ref, bias_ref)

    @pl.when((pl.program_id(0) == 0) & (pl.program_id(1) == 0))
    def _():
        zeros = jnp.zeros((CONV_CARRY_ROWS, x_ref.shape[2]), F32)
        om_ref[...], meta_carry_ref[...] = _conv_rows(m_ref[...], zeros, *weights)

    @pl.when(pl.program_id(1) == 0)
    def _():
        carry_ref[...] = meta_carry_ref[...]

    prev = carry_ref[...]
    for s in range(CONV_SUBTILES):
        rows = slice(s * CONV_SUB_TILE, (s + 1) * CONV_SUB_TILE)
        ox_ref[0, rows, :], prev = _conv_rows(x_ref[0, rows, :], prev, *weights)
    carry_ref[...] = prev


def _conv_block(hx, hm, w_in, w_conv, w_out, gain, bias):
    bsz, seq, d = hx.shape
    tm = CONV_SUB_TILE * CONV_SUBTILES
    assert seq % tm == 0 and CONV_WIDTH - 1 <= CONV_CARRY_ROWS <= N_META
    x_spec = pl.BlockSpec((1, tm, d), lambda b, t: (b, t, 0))
    return pl.pallas_call(
        _conv_kernel,
        grid=(bsz, seq // tm),
        in_specs=[x_spec, _resident((N_META, d)), _resident((d, 3 * d)), _resident((CONV_WIDTH, d)),
                  _resident((d, d)), _resident((1, d)), _resident((1, d))],
        out_specs=(x_spec, pl.BlockSpec((N_META, d), lambda b, t: (0, 0))),
        out_shape=(jax.ShapeDtypeStruct((bsz, seq, d), F32), jax.ShapeDtypeStruct((N_META, d), F32)),
        scratch_shapes=[pltpu.VMEM((CONV_CARRY_ROWS, d), F32), pltpu.VMEM((CONV_CARRY_ROWS, d), F32)],
        compiler_params=pltpu.CompilerParams(
            dimension_semantics=("arbitrary", "arbitrary"), vmem_limit_bytes=VMEM_LIMIT_BYTES),
        name="conv_block",
    )(hx, hm, w_in, w_conv, w_out, gain, bias)


def _split3_bf16(a):
    hi = a.astype(BF16)
    r = a - hi.astype(F32)
    mid = r.astype(BF16)
    lo = (r - mid.astype(F32)).astype(BF16)
    return hi, mid, lo


def _gla_rows(x, c, state_ref, weights, scratch):
    (w_in_ref, w_gu_ref, b_gate_ref, norm_w_ref, w_out_ref, gain_ref, bias_ref) = weights
    (q_ref, k_ref, v_ref, la_ref, b_ref, og_ref, qin_ref, qt_ref, kt_ref, kst_ref, kstT_ref,
     p_ref, upd_ref, gate_ref) = scratch
    n = x.shape[0]
    n_chunks = n // c
    all_rows = slice(0, n)
    o_k, o_v, o_g, o_gz = GLA_DK, 2 * GLA_DK, 2 * GLA_DK + GLA_DV, 2 * GLA_DK + 2 * GLA_DV
    xb = x.astype(BF16)
    gz = _dot(xb, w_in_ref[:, o_gz:])
    q_ref[all_rows, :] = _dot(xb, w_in_ref[:, :o_k]) * (GLA_HEAD_K ** -0.5)
    z = _dot(gz.astype(BF16), w_gu_ref[...]) + b_gate_ref[...]
    k_ref[all_rows, :] = _dot(xb, w_in_ref[:, o_k:o_v])
    v_ref[all_rows, :] = _dot(xb, w_in_ref[:, o_v:o_g]).astype(BF16)
    la_ref[all_rows, :] = (jnp.minimum(z, 0.0) - jnp.log1p(jnp.exp(-jnp.abs(z)))) * (1.0 / GLA_GATE_NORMALIZER)

    ri = lax.broadcasted_iota(jnp.int32, (c, c), 0)
    ci = lax.broadcasted_iota(jnp.int32, (c, c), 1)
    causal = ci <= ri
    tril = causal.astype(BF16)
    head_k = [slice(hd * GLA_HEAD_K, (hd + 1) * GLA_HEAD_K) for hd in range(GLA_HEADS)]
    head_v = [slice(hd * GLA_HEAD_V, (hd + 1) * GLA_HEAD_V) for hd in range(GLA_HEADS)]
    chunk_rows = [slice(ic * c, (ic + 1) * c) for ic in range(n_chunks)]

    for rows in chunk_rows:
        hi, mid, lo = _split3_bf16(la_ref[rows, :])
        b_ref[rows, :] = _dot(tril, hi) + _dot(tril, mid) + _dot(tril, lo)
    gate = _dot(xb, w_in_ref[:, o_g:o_gz])
    gate_ref[all_rows, :] = gate * jax.nn.sigmoid(gate)
    decays = []
    for rows in chunk_rows:
        b = b_ref[rows, :]
        b_mid = b[c // 2:c // 2 + 1]
        b_last = b[c - 1:c]
        q_t = q_ref[rows, :] * jnp.exp(b - b_mid)
        k_t = k_ref[rows, :] * jnp.exp(b_mid - b)
        qin_ref[rows, :] = (q_t * jnp.exp(b_mid)).astype(BF16)
        kst_ref[rows, :] = (k_t * jnp.exp(b_last - b_mid)).astype(BF16)
        qt_ref[rows, :] = q_t.astype(BF16)
        kt_ref[rows, :] = k_t.astype(BF16)
        decays.append(jnp.exp(b_last))
    for ic, rows in enumerate(chunk_rows):
        kstT_ref[ic, :, 0:c] = jnp.transpose(kst_ref[rows, :])
    for ic, rows in enumerate(chunk_rows):
        for hd in range(GLA_HEADS):
            scores = lax.dot_general(qt_ref[rows, head_k[hd]], kt_ref[rows, head_k[hd]],
                                     (((1,), (1,)), ((), ())), preferred_element_type=F32)
            p_ref[hd, rows, 0:c] = jnp.where(causal, scores, 0.0).astype(BF16)
            upd_ref[ic, hd] = _dot(kstT_ref[ic, head_k[hd], 0:c], v_ref[rows, head_v[hd]])

    pad = [jnp.zeros_like(decays[0])] * (-n_chunks % 8)
    dec_rows = jnp.concatenate(decays + pad, axis=0)
    dec_cols = [jnp.transpose(dec_rows[:, head_k[hd]]) for hd in range(GLA_HEADS)]

    norm_w = norm_w_ref[...]
    for ic, rows in enumerate(chunk_rows):
        for hd in range(GLA_HEADS):
            s_prev = state_ref[hd]
            o = (_dot(p_ref[hd, rows, 0:c], v_ref[rows, head_v[hd]])
                 + _dot(qin_ref[rows, head_k[hd]], s_prev.astype(BF16)))
            state_ref[hd] = dec_cols[hd][:, ic:ic + 1] * s_prev + upd_ref[ic, hd]
            o = o * lax.rsqrt(jnp.mean(o * o, axis=-1, keepdims=True) + RMS_EPS) * norm_w
            og_ref[rows, head_v[hd]] = (o * gate_ref[rows, head_v[hd]]).astype(BF16)

    halves = (slice(0, n // 2), slice(n // 2, n)) if n >= 2 * GLA_OUT_MIN_ROWS else (all_rows,)
    outs = []
    for r in halves:
        mix = _dot(og_ref[r, :], w_out_ref[...])
        outs.append(_layer_norm(DEEPNORM_ALPHA * x[r] + mix, gain_ref[...], bias_ref[...]))
    return jnp.concatenate(outs, axis=0)


def _gla_kernel(x_ref, m_ref, w_in_ref, w_gu_ref, b_gate_ref, norm_w_ref, w_out_ref, gain_ref, bias_ref,
                ox_ref, om_ref, state_ref, meta_state_ref, *scratch):
    weights = (w_in_ref, w_gu_ref, b_gate_ref, norm_w_ref, w_out_ref, gain_ref, bias_ref)

    @pl.when((pl.program_id(0) == 0) & (pl.program_id(1) == 0))
    def _():
        meta_state_ref[...] = jnp.zeros_like(meta_state_ref)
        om_ref[...] = _gla_rows(m_ref[...], N_META, meta_state_ref, weights, scratch)

    @pl.when(pl.program_id(1) == 0)
    def _():
        state_ref[...] = meta_state_ref[...]

    ox_ref[0] = _gla_rows(x_ref[0], GLA_CHUNK, state_ref, weights, scratch)


def _gla_block(hx, hm, w_in, w_gu, b_gate, norm_w, w_out, gain, bias):
    bsz, seq, d = hx.shape
    tm = GLA_TILE
    assert seq % tm == 0 and tm % GLA_CHUNK == 0 and N_META <= GLA_CHUNK
    n_chunks = tm // GLA_CHUNK
    x_spec = pl.BlockSpec((1, tm, d), lambda b, t: (b, t, 0))
    state = pltpu.VMEM((GLA_HEADS, GLA_HEAD_K, GLA_HEAD_V), F32)
    return pl.pallas_call(
        _gla_kernel,
        grid=(bsz, seq // tm),
        in_specs=[x_spec, _resident((N_META, d)), _resident(w_in.shape), _resident((GLA_GATE_RANK, GLA_DK)),
                  _resident((1, GLA_DK)), _resident((1, GLA_HEAD_V)), _resident((GLA_DV, d)),
                  _resident((1, d)), _resident((1, d))],
        out_specs=(x_spec, pl.BlockSpec((N_META, d), lambda b, t: (0, 0))),
        out_shape=(jax.ShapeDtypeStruct((bsz, seq, d), F32), jax.ShapeDtypeStruct((N_META, d), F32)),
        scratch_shapes=[
            state, state,
            pltpu.VMEM((tm, GLA_DK), F32),
            pltpu.VMEM((tm, GLA_DK), F32),
            pltpu.VMEM((tm, GLA_DV), BF16),
            pltpu.VMEM((tm, GLA_DK), F32),
            pltpu.VMEM((tm, GLA_DK), F32),
            pltpu.VMEM((tm, GLA_DV), BF16),
            pltpu.VMEM((tm, GLA_DK), BF16),
            pltpu.VMEM((tm, GLA_DK), BF16),
            pltpu.VMEM((tm, GLA_DK), BF16),
            pltpu.VMEM((tm, GLA_DK), BF16),
            pltpu.VMEM((n_chunks, GLA_DK, GLA_CHUNK), BF16),
            pltpu.VMEM((GLA_HEADS, tm, GLA_CHUNK), BF16),
            pltpu.VMEM((n_chunks, GLA_HEADS, GLA_HEAD_K, GLA_HEAD_V), F32),
            pltpu.VMEM((tm, GLA_DV), F32),
        ],
        compiler_params=pltpu.CompilerParams(
            dimension_semantics=("arbitrary", "arbitrary"), vmem_limit_bytes=VMEM_LIMIT_BYTES),
        name="gla_block",
    )(hx, hm, w_in, w_gu, b_gate, norm_w, w_out, gain, bias)


def kernel(x, meta_tokens, ln_gain, ln_bias, ffn_w_in, ffn_w_out, conv_w_in, conv_w, conv_w_out,
           gla_w_in, gla_w_gate_up, gla_b_gate, gla_norm_w, gla_w_out):
    bsz, seq, d = x.shape
    hx = x
    hm = meta_tokens.astype(x.dtype)

    def ffn(hx, hm, i, half):
        ox, om = _ffn_block(hx.reshape(bsz * seq, d), hm, ffn_w_in, ffn_w_out,
                            ln_gain[i, 2 * half][None], ln_bias[i, 2 * half][None], (i, half))
        return ox.reshape(bsz, seq, d), om

    for i in range(DEPTH):
        hx, hm = ffn(hx, hm, i, 0)
        j = i // 2
        gain, bias = ln_gain[i, 1][None], ln_bias[i, 1][None]
        if i % 2 == 0:
            hx, hm = _conv_block(hx, hm, conv_w_in[j], conv_w[j], conv_w_out[j], gain, bias)
        else:
            hx, hm = _gla_block(hx, hm, gla_w_in[j].astype(BF16), gla_w_gate_up[j].astype(BF16),
                                gla_b_gate[j][None], gla_norm_w[j][None], gla_w_out[j].astype(BF16), gain, bias)
        hx, hm = ffn(hx, hm if i + 1 < DEPTH else None, i, 1)
    return hx
```

```python
import jax
import jax.numpy as jnp
from jax import lax
from jax.experimental import pallas as pl
from jax.experimental.pallas import tpu as pltpu

D_MODEL = 1024
DEPTH = 2
N_META = 16
CONV_WIDTH = 3
GLA_HEADS = 4
GLA_DK = D_MODEL // 2
GLA_DV = D_MODEL
GLA_HEAD_K = GLA_DK // GLA_HEADS
GLA_HEAD_V = GLA_DV // GLA_HEADS
GLA_GATE_RANK = 16
GLA_GATE_NORMALIZER = 16.0
GLA_CHUNK = 64
DEEPNORM_ALPHA = (2.0 * DEPTH) ** 0.25
LN_EPS = 1e-5
RMS_EPS = 1e-6

CONV_CARRY_ROWS = 8
FFN_TILE = 512
CONV_SUB_TILE = 256
CONV_SUBTILES = 4
GLA_TILE = 512
GLA_OUT_MIN_ROWS = 256
VMEM_LIMIT_BYTES = 56 * 1024 * 1024

F32 = jnp.float32
BF16 = jnp.bfloat16


def _dot(a, b):
    return jnp.dot(a, b, preferred_element_type=F32)


def _layer_norm(y, gain, bias):
    mu = jnp.mean(y, axis=-1, keepdims=True)
    yc = y - mu
    var = jnp.mean(yc * yc, axis=-1, keepdims=True)
    return yc * lax.rsqrt(var + LN_EPS) * gain + bias


def _resident(shape):
    zeros = (0,) * len(shape)
    return pl.BlockSpec(shape, lambda *_: zeros, pipeline_mode=pl.Buffered(1))


def _ffn_residual(x, w_in_ref, w_out_ref, before_gate=None, before_up=None, before_out=None):
    f = w_out_ref.shape[0]
    xb = x.astype(BF16)
    if before_gate is not None:
        before_gate()
    gate = _dot(xb, w_in_ref[:, :f])
    if before_up is not None:
        before_up()
    up = _dot(xb, w_in_ref[:, f:])
    act = (gate * jax.nn.sigmoid(gate) * up).astype(BF16)
    if before_out is not None:
        before_out()
    return DEEPNORM_ALPHA * x + 0.5 * _dot(act, w_out_ref[...])


def _ffn_kernel(with_meta, n_tiles, sel, *refs):
    if with_meta:
        (x_ref, m_ref, w_in_hbm, w_out_hbm, gain_ref, bias_ref, ox_ref, om_ref,
         y_ref, w_in_ref, w_out_ref, sem) = refs
    else:
        x_ref, w_in_hbm, w_out_hbm, gain_ref, bias_ref, ox_ref, y_ref, w_in_ref, w_out_ref, sem = refs
    i = pl.program_id(0)
    f = w_out_ref.shape[0]
    layer, half = sel
    copies = (
        pltpu.make_async_copy(w_in_hbm.at[layer, half, :, pl.ds(0, f)], w_in_ref.at[:, pl.ds(0, f)], sem.at[0]),
        pltpu.make_async_copy(w_in_hbm.at[layer, half, :, pl.ds(f, f)], w_in_ref.at[:, pl.ds(f, f)], sem.at[1]),
        pltpu.make_async_copy(w_out_hbm.at[layer, half], w_out_ref, sem.at[2]),
    )

    def normalise_previous():
        ox_ref[...] = _layer_norm(y_ref[...], gain_ref[...], bias_ref[...])

    def form_current(*hooks):
        y_ref[...] = _ffn_residual(x_ref[...], w_in_ref, w_out_ref, *hooks)

    @pl.when(i == 0)
    def _():
        for copy in copies:
            copy.start()
        form_current(*(copy.wait for copy in copies))
        if with_meta:
            om_ref[...] = _layer_norm(_ffn_residual(m_ref[...], w_in_ref, w_out_ref),
                                      gain_ref[...], bias_ref[...])

    @pl.when((i > 0) & (i < n_tiles))
    def _():
        normalise_previous()
        form_current()

    @pl.when(i == n_tiles)
    def _():
        normalise_previous()


def _ffn_block(hx, hm, w_in, w_out, gain, bias, sel):
    n, d = hx.shape
    f = w_out.shape[2]
    tm = FFN_TILE
    assert n % tm == 0
    n_tiles = n // tm
    with_meta = hm is not None
    x_spec = pl.BlockSpec((tm, d), lambda i: (jnp.minimum(i, n_tiles - 1), 0))
    o_spec = pl.BlockSpec((tm, d), lambda i: (jnp.maximum(i - 1, 0), 0))
    hbm = pl.BlockSpec(memory_space=pl.ANY)
    w_specs = [hbm, hbm, _resident((1, d)), _resident((1, d))]
    params = pltpu.CompilerParams(dimension_semantics=("arbitrary",), vmem_limit_bytes=VMEM_LIMIT_BYTES)
    scratch = [pltpu.VMEM((tm, d), F32),
               pltpu.VMEM((d, 2 * f), F32), pltpu.VMEM((f, d), F32),
               pltpu.SemaphoreType.DMA((3,))]
    if not with_meta:
        return pl.pallas_call(
            lambda *refs: _ffn_kernel(False, n_tiles, sel, *refs),
            grid=(n_tiles + 1,), in_specs=[x_spec] + w_specs, out_specs=o_spec,
            out_shape=jax.ShapeDtypeStruct((n, d), F32), scratch_shapes=scratch,
            compiler_params=params, name="ffn_block",
        )(hx, w_in, w_out, gain, bias), None
    return pl.pallas_call(
        lambda *refs: _ffn_kernel(True, n_tiles, sel, *refs),
        grid=(n_tiles + 1,), in_specs=[x_spec, _resident((N_META, d))] + w_specs,
        out_specs=(o_spec, pl.BlockSpec((N_META, d), lambda i: (0, 0))),
        out_shape=(jax.ShapeDtypeStruct((n, d), F32), jax.ShapeDtypeStruct((N_META, d), F32)),
        scratch_shapes=scratch, compiler_params=params, name="ffn_meta_block",
    )(hx, hm, w_in, w_out, gain, bias)


def _conv_rows(x, prev, w_in_ref, w_conv_ref, w_out_ref, gain_ref, bias_ref):
    n, d = x.shape
    xb = x.astype(BF16)
    u = _dot(xb, w_in_ref[:, d:2 * d]) * _dot(xb, w_in_ref[:, 2 * d:])
    bgate = _dot(xb, w_in_ref[:, :d])
    row = lax.broadcasted_iota(jnp.int32, (n, 1), 0)
    last = prev[CONV_CARRY_ROWS - 1:CONV_CARRY_ROWS]
    u1 = jnp.where(row == 0, last, pltpu.roll(u, 1, 0))
    u2 = jnp.where(row == 0, prev[CONV_CARRY_ROWS - 2:CONV_CARRY_ROWS - 1],
                   jnp.where(row == 1, last, pltpu.roll(u, 2, 0)))
    w = w_conv_ref[...]
    conv = w[0:1] * u2 + w[1:2] * u1 + w[2:3] * u
    mix = _dot((bgate * conv).astype(BF16), w_out_ref[...])
    out = _layer_norm(DEEPNORM_ALPHA * x + mix, gain_ref[...], bias_ref[...])
    return out, u[n - CONV_CARRY_ROWS:]


def _conv_kernel(x_ref, m_ref, w_in_ref, w_conv_ref, w_out_ref, gain_ref, bias_ref,
                 ox_ref, om_ref, carry_ref, meta_carry_ref):
    weights = (w_in_ref, w_conv_ref, w_out_ref, gain_ref, bias_ref)

    @pl.when((pl.program_id(0) == 0) & (pl.program_id(1) == 0))
    def _():
        zeros = jnp.zeros((CONV_CARRY_ROWS, x_ref.shape[2]), F32)
        om_ref[...], meta_carry_ref[...] = _conv_rows(m_ref[...], zeros, *weights)

    @pl.when(pl.program_id(1) == 0)
    def _():
        carry_ref[...] = meta_carry_ref[...]

    prev = carry_ref[...]
    for s in range(CONV_SUBTILES):
        rows = slice(s * CONV_SUB_TILE, (s + 1) * CONV_SUB_TILE)
        ox_ref[0, rows, :], prev = _conv_rows(x_ref[0, rows, :], prev, *weights)
    carry_ref[...] = prev


def _conv_block(hx, hm, w_in, w_conv, w_out, gain, bias):
    bsz, seq, d = hx.shape
    tm = CONV_SUB_TILE * CONV_SUBTILES
    assert seq % tm == 0 and CONV_WIDTH - 1 <= CONV_CARRY_ROWS <= N_META
    x_spec = pl.BlockSpec((1, tm, d), lambda b, t: (b, t, 0))
    return pl.pallas_call(
        _conv_kernel,
        grid=(bsz, seq // tm),
        in_specs=[x_spec, _resident((N_META, d)), _resident((d, 3 * d)), _resident((CONV_WIDTH, d)),
                  _resident((d, d)), _resident((1, d)), _resident((1, d))],
        out_specs=(x_spec, pl.BlockSpec((N_META, d), lambda b, t: (0, 0))),
        out_shape=(jax.ShapeDtypeStruct((bsz, seq, d), F32), jax.ShapeDtypeStruct((N_META, d), F32)),
        scratch_shapes=[pltpu.VMEM((CONV_CARRY_ROWS, d), F32), pltpu.VMEM((CONV_CARRY_ROWS, d), F32)],
        compiler_params=pltpu.CompilerParams(
            dimension_semantics=("arbitrary", "arbitrary"), vmem_limit_bytes=VMEM_LIMIT_BYTES),
        name="conv_block",
    )(hx, hm, w_in, w_conv, w_out, gain, bias)


def _split3_bf16(a):
    hi = a.astype(BF16)
    r = a - hi.astype(F32)
    mid = r.astype(BF16)
    lo = (r - mid.astype(F32)).astype(BF16)
    return hi, mid, lo


def _gla_rows(x, c, state_ref, weights, scratch):
    (w_in_ref, w_gu_ref, b_gate_ref, norm_w_ref, w_out_ref, gain_ref, bias_ref) = weights
    (q_ref, k_ref, v_ref, la_ref, b_ref, og_ref, qin_ref, qt_ref, kt_ref, kst_ref, kstT_ref,
     p_ref, upd_ref, gate_ref) = scratch
    n = x.shape[0]
    n_chunks = n // c
    all_rows = slice(0, n)
    o_k, o_v, o_g, o_gz = GLA_DK, 2 * GLA_DK, 2 * GLA_DK + GLA_DV, 2 * GLA_DK + 2 * GLA_DV
    xb = x.astype(BF16)
    gz = _dot(xb, w_in_ref[:, o_gz:])
    q_ref[all_rows, :] = _dot(xb, w_in_ref[:, :o_k]) * (GLA_HEAD_K ** -0.5)
    z = _dot(gz.astype(BF16), w_gu_ref[...]) + b_gate_ref[...]
    k_ref[all_rows, :] = _dot(xb, w_in_ref[:, o_k:o_v])
    v_ref[all_rows, :] = _dot(xb, w_in_ref[:, o_v:o_g]).astype(BF16)
    la_ref[all_rows, :] = (jnp.minimum(z, 0.0) - jnp.log1p(jnp.exp(-jnp.abs(z)))) * (1.0 / GLA_GATE_NORMALIZER)

    ri = lax.broadcasted_iota(jnp.int32, (c, c), 0)
    ci = lax.broadcasted_iota(jnp.int32, (c, c), 1)
    causal = ci <= ri
    tril = causal.astype(BF16)
    head_k = [slice(hd * GLA_HEAD_K, (hd + 1) * GLA_HEAD_K) for hd in range(GLA_HEADS)]
    head_v = [slice(hd * GLA_HEAD_V, (hd + 1) * GLA_HEAD_V) for hd in range(GLA_HEADS)]
    chunk_rows = [slice(ic * c, (ic + 1) * c) for ic in range(n_chunks)]

    for rows in chunk_rows:
        hi, mid, lo = _split3_bf16(la_ref[rows, :])
        b_ref[rows, :] = _dot(tril, hi) + _dot(tril, mid) + _dot(tril, lo)
    gate = _dot(xb, w_in_ref[:, o_g:o_gz])
    gate_ref[all_rows, :] = gate * jax.nn.sigmoid(gate)
    decays = []
    for rows in chunk_rows:
        b = b_ref[rows, :]
        b_mid = b[c // 2:c // 2 + 1]
        b_last = b[c - 1:c]
        q_t = q_ref[rows, :] * jnp.exp(b - b_mid)
        k_t = k_ref[rows, :] * jnp.exp(b_mid - b)
        qin_ref[rows, :] = (q_t * jnp.exp(b_mid)).astype(BF16)
        kst_ref[rows, :] = (k_t * jnp.exp(b_last - b_mid)).astype(BF16)
        qt_ref[rows, :] = q_t.astype(BF16)
        kt_ref[rows, :] = k_t.astype(BF16)
        decays.append(jnp.exp(b_last))
    for ic, rows in enumerate(chunk_rows):
        kstT_ref[ic, :, 0:c] = jnp.transpose(kst_ref[rows, :])
    for ic, rows in enumerate(chunk_rows):
        for hd in range(GLA_HEADS):
            scores = lax.dot_general(qt_ref[rows, head_k[hd]], kt_ref[rows, head_k[hd]],
                                     (((1,), (1,)), ((), ())), preferred_element_type=F32)
            p_ref[hd, rows, 0:c] = jnp.where(causal, scores, 0.0).astype(BF16)
            upd_ref[ic, hd] = _dot(kstT_ref[ic, head_k[hd], 0:c], v_ref[rows, head_v[hd]])

    pad = [jnp.zeros_like(decays[0])] * (-n_chunks % 8)
    dec_rows = jnp.concatenate(decays + pad, axis=0)
    dec_cols = [jnp.transpose(dec_rows[:, head_k[hd]]) for hd in range(GLA_HEADS)]

    norm_w = norm_w_ref[...]
    for ic, rows in enumerate(chunk_rows):
        for hd in range(GLA_HEADS):
            s_prev = state_ref[hd]
            o = (_dot(p_ref[hd, rows, 0:c], v_ref[rows, head_v[hd]])
                 + _dot(qin_ref[rows, head_k[hd]], s_prev.astype(BF16)))
            state_ref[hd] = dec_cols[hd][:, ic:ic + 1] * s_prev + upd_ref[ic, hd]
            o = o * lax.rsqrt(jnp.mean(o * o, axis=-1, keepdims=True) + RMS_EPS) * norm_w
            og_ref[rows, head_v[hd]] = (o * gate_ref[rows, head_v[hd]]).astype(BF16)

    halves = (slice(0, n // 2), slice(n // 2, n)) if n >= 2 * GLA_OUT_MIN_ROWS else (all_rows,)
    outs = []
    for r in halves:
        mix = _dot(og_ref[r, :], w_out_ref[...])
        outs.append(_layer_norm(DEEPNORM_ALPHA * x[r] + mix, gain_ref[...], bias_ref[...]))
    return jnp.concatenate(outs, axis=0)


def _gla_kernel(x_ref, m_ref, w_in_ref, w_gu_ref, b_gate_ref, norm_w_ref, w_out_ref, gain_ref, bias_ref,
                ox_ref, om_ref, state_ref, meta_state_ref, *scratch):
    weights = (w_in_ref, w_gu_ref, b_gate_ref, norm_w_ref, w_out_ref, gain_ref, bias_ref)

    @pl.when((pl.program_id(0) == 0) & (pl.program_id(1) == 0))
    def _():
        meta_state_ref[...] = jnp.zeros_like(meta_state_ref)
        om_ref[...] = _gla_rows(m_ref[...], N_META, meta_state_ref, weights, scratch)

    @pl.when(pl.program_id(1) == 0)
    def _():
        state_ref[...] = meta_state_ref[...]

    ox_ref[0] = _gla_rows(x_ref[0], GLA_CHUNK, state_ref, weights, scratch)


def _gla_block(hx, hm, w_in, w_gu, b_gate, norm_w, w_out, gain, bias):
    bsz, seq, d = hx.shape
    tm = GLA_TILE
    assert seq % tm == 0 and tm % GLA_CHUNK == 0 and N_META <= GLA_CHUNK
    n_chunks = tm // GLA_CHUNK
    x_spec = pl.BlockSpec((1, tm, d), lambda b, t: (b, t, 0))
    state = pltpu.VMEM((GLA_HEADS, GLA_HEAD_K, GLA_HEAD_V), F32)
    return pl.pallas_call(
        _gla_kernel,
        grid=(bsz, seq // tm),
        in_specs=[x_spec, _resident((N_META, d)), _resident(w_in.shape), _resident((GLA_GATE_RANK, GLA_DK)),
                  _resident((1, GLA_DK)), _resident((1, GLA_HEAD_V)), _resident((GLA_DV, d)),
                  _resident((1, d)), _resident((1, d))],
        out_specs=(x_spec, pl.BlockSpec((N_META, d), lambda b, t: (0, 0))),
        out_shape=(jax.ShapeDtypeStruct((bsz, seq, d), F32), jax.ShapeDtypeStruct((N_META, d), F32)),
        scratch_shapes=[
            state, state,
            pltpu.VMEM((tm, GLA_DK), F32),
            pltpu.VMEM((tm, GLA_DK), F32),
            pltpu.VMEM((tm, GLA_DV), BF16),
            pltpu.VMEM((tm, GLA_DK), F32),
            pltpu.VMEM((tm, GLA_DK), F32),
            pltpu.VMEM((tm, GLA_DV), BF16),
            pltpu.VMEM((tm, GLA_DK), BF16),
            pltpu.VMEM((tm, GLA_DK), BF16),
            pltpu.VMEM((tm, GLA_DK), BF16),
            pltpu.VMEM((tm, GLA_DK), BF16),
            pltpu.VMEM((n_chunks, GLA_DK, GLA_CHUNK), BF16),
            pltpu.VMEM((GLA_HEADS, tm, GLA_CHUNK), BF16),
            pltpu.VMEM((n_chunks, GLA_HEADS, GLA_HEAD_K, GLA_HEAD_V), F32),
            pltpu.VMEM((tm, GLA_DV), F32),
        ],
        compiler_params=pltpu.CompilerParams(
            dimension_semantics=("arbitrary", "arbitrary"), vmem_limit_bytes=VMEM_LIMIT_BYTES),
        name="gla_block",
    )(hx, hm, w_in, w_gu, b_gate, norm_w, w_out, gain, bias)


def kernel(x, meta_tokens, ln_gain, ln_bias, ffn_w_in, ffn_w_out, conv_w_in, conv_w, conv_w_out,
           gla_w_in, gla_w_gate_up, gla_b_gate, gla_norm_w, gla_w_out):
    bsz, seq, d = x.shape
    hx = x
    hm = meta_tokens.astype(x.dtype)

    def ffn(hx, hm, i, half):
        ox, om = _ffn_block(hx.reshape(bsz * seq, d), hm, ffn_w_in, ffn_w_out,
                            ln_gain[i, 2 * half][None], ln_bias[i, 2 * half][None], (i, half))
        return ox.reshape(bsz, seq, d), om

    for i in range(DEPTH):
        hx, hm = ffn(hx, hm, i, 0)
        j = i // 2
        gain, bias = ln_gain[i, 1][None], ln_bias[i, 1][None]
        if i % 2 == 0:
            hx, hm = _conv_block(hx, hm, conv_w_in[j].astype(BF16), conv_w[j], conv_w_out[j].astype(BF16),
                                 gain, bias)
        else:
            hx, hm = _gla_block(hx, hm, gla_w_in[j].astype(BF16), gla_w_gate_up[j].astype(BF16),
                                gla_b_gate[j][None], gla_norm_w[j][None], gla_w_out[j].astype(BF16), gain, bias)
        hx, hm = ffn(hx, hm if i + 1 < DEPTH else None, i, 1)
    return hx
```

```python
import jax
import jax.numpy as jnp
from jax import lax
from jax.experimental import pallas as pl
from jax.experimental.pallas import tpu as pltpu

D_MODEL = 1024
DEPTH = 2
N_META = 16
CONV_WIDTH = 3
GLA_HEADS = 4
GLA_DK = D_MODEL // 2
GLA_DV = D_MODEL
GLA_HEAD_K = GLA_DK // GLA_HEADS
GLA_HEAD_V = GLA_DV // GLA_HEADS
GLA_GATE_RANK = 16
GLA_GATE_NORMALIZER = 16.0
GLA_CHUNK = 64
DEEPNORM_ALPHA = (2.0 * DEPTH) ** 0.25
LN_EPS = 1e-5
RMS_EPS = 1e-6

CONV_CARRY_ROWS = 8
FFN_TILE = 512
CONV_SUB_TILE = 256
CONV_SUBTILES = 4
GLA_TILE = 512
GLA_OUT_MIN_ROWS = 256
VMEM_LIMIT_BYTES = 56 * 1024 * 1024

F32 = jnp.float32
BF16 = jnp.bfloat16


def _dot(a, b):
    return jnp.dot(a, b, preferred_element_type=F32)


def _layer_norm(y, gain, bias):
    mu = jnp.mean(y, axis=-1, keepdims=True)
    yc = y - mu
    var = jnp.mean(yc * yc, axis=-1, keepdims=True)
    return yc * lax.rsqrt(var + LN_EPS) * gain + bias


def _resident(shape):
    zeros = (0,) * len(shape)
    return pl.BlockSpec(shape, lambda *_: zeros, pipeline_mode=pl.Buffered(1))


def _ffn_residual(x, w_in_ref, w_out_ref, before_gate=None, before_up=None, before_out=None):
    f = w_out_ref.shape[0]
    xb = x.astype(BF16)
    if before_gate is not None:
        before_gate()
    gate = _dot(xb, w_in_ref[:, :f])
    if before_up is not None:
        before_up()
    up = _dot(xb, w_in_ref[:, f:])
    act = (gate * jax.nn.sigmoid(gate) * up).astype(BF16)
    if before_out is not None:
        before_out()
    return DEEPNORM_ALPHA * x + 0.5 * _dot(act, w_out_ref[...])


def _ffn_kernel(with_meta, n_tiles, sel, *refs):
    if with_meta:
        (x_ref, m_ref, w_in_hbm, w_out_hbm, gain_ref, bias_ref, ox_ref, om_ref,
         y_ref, w_in_ref, w_out_ref, sem) = refs
    else:
        x_ref, w_in_hbm, w_out_hbm, gain_ref, bias_ref, ox_ref, y_ref, w_in_ref, w_out_ref, sem = refs
    i = pl.program_id(0)
    f = w_out_ref.shape[0]
    layer, half = sel
    copies = (
        pltpu.make_async_copy(w_in_hbm.at[layer, half, :, pl.ds(0, f)], w_in_ref.at[:, pl.ds(0, f)], sem.at[0]),
        pltpu.make_async_copy(w_in_hbm.at[layer, half, :, pl.ds(f, f)], w_in_ref.at[:, pl.ds(f, f)], sem.at[1]),
        pltpu.make_async_copy(w_out_hbm.at[layer, half], w_out_ref, sem.at[2]),
    )

    def normalise_previous():
        ox_ref[...] = _layer_norm(y_ref[...], gain_ref[...], bias_ref[...])

    def form_current(*hooks):
        y_ref[...] = _ffn_residual(x_ref[...], w_in_ref, w_out_ref, *hooks)

    @pl.when(i == 0)
    def _():
        for copy in copies:
            copy.start()
        form_current(*(copy.wait for copy in copies))
        if with_meta:
            om_ref[...] = _layer_norm(_ffn_residual(m_ref[...], w_in_ref, w_out_ref),
                                      gain_ref[...], bias_ref[...])

    @pl.when((i > 0) & (i < n_tiles))
    def _():
        normalise_previous()
        form_current()

    @pl.when(i == n_tiles)
    def _():
        normalise_previous()


def _ffn_block(hx, hm, w_in, w_out, gain, bias, sel):
    n, d = hx.shape
    f = w_out.shape[2]
    tm = FFN_TILE
    assert n % tm == 0
    n_tiles = n // tm
    with_meta = hm is not None
    x_spec = pl.BlockSpec((tm, d), lambda i: (jnp.minimum(i, n_tiles - 1), 0))
    o_spec = pl.BlockSpec((tm, d), lambda i: (jnp.maximum(i - 1, 0), 0))
    hbm = pl.BlockSpec(memory_space=pl.ANY)
    w_specs = [hbm, hbm, _resident((1, d)), _resident((1, d))]
    params = pltpu.CompilerParams(dimension_semantics=("arbitrary",), vmem_limit_bytes=VMEM_LIMIT_BYTES)
    scratch = [pltpu.VMEM((tm, d), F32),
               pltpu.VMEM((d, 2 * f), w_in.dtype), pltpu.VMEM((f, d), w_out.dtype),
               pltpu.SemaphoreType.DMA((3,))]
    if not with_meta:
        return pl.pallas_call(
            lambda *refs: _ffn_kernel(False, n_tiles, sel, *refs),
            grid=(n_tiles + 1,), in_specs=[x_spec] + w_specs, out_specs=o_spec,
            out_shape=jax.ShapeDtypeStruct((n, d), F32), scratch_shapes=scratch,
            compiler_params=params, name="ffn_block",
        )(hx, w_in, w_out, gain, bias), None
    return pl.pallas_call(
        lambda *refs: _ffn_kernel(True, n_tiles, sel, *refs),
        grid=(n_tiles + 1,), in_specs=[x_spec, _resident((N_META, d))] + w_specs,
        out_specs=(o_spec, pl.BlockSpec((N_META, d), lambda i: (0, 0))),
        out_shape=(jax.ShapeDtypeStruct((n, d), F32), jax.ShapeDtypeStruct((N_META, d), F32)),
        scratch_shapes=scratch, compiler_params=params, name="ffn_meta_block",
    )(hx, hm, w_in, w_out, gain, bias)


def _conv_rows(x, prev, w_in_ref, w_conv_ref, w_out_ref, gain_ref, bias_ref):
    n, d = x.shape
    xb = x.astype(BF16)
    u = _dot(xb, w_in_ref[:, d:2 * d]) * _dot(xb, w_in_ref[:, 2 * d:])
    bgate = _dot(xb, w_in_ref[:, :d])
    row = lax.broadcasted_iota(jnp.int32, (n, 1), 0)
    last = prev[CONV_CARRY_ROWS - 1:CONV_CARRY_ROWS]
    u1 = jnp.where(row == 0, last, pltpu.roll(u, 1, 0))
    u2 = jnp.where(row == 0, prev[CONV_CARRY_ROWS - 2:CONV_CARRY_ROWS - 1],
                   jnp.where(row == 1, last, pltpu.roll(u, 2, 0)))
    w = w_conv_ref[...]
    conv = w[0:1] * u2 + w[1:2] * u1 + w[2:3] * u
    mix = _dot((bgate * conv).astype(BF16), w_out_ref[...])
    out = _layer_norm(DEEPNORM_ALPHA * x + mix, gain_ref[...], bias_ref[...])
    return out, u[n - CONV_CARRY_ROWS:]


def _conv_kernel(x_ref, m_ref, w_in_ref, w_conv_ref, w_out_ref, gain_ref, bias_ref,
                 ox_ref, om_ref, carry_ref, meta_carry_ref):
    weights = (w_in_ref, w_conv_ref, w_out_ref, gain_ref, bias_ref)

    @pl.when((pl.program_id(0) == 0) & (pl.program_id(1) == 0))
    def _():
        zeros = jnp.zeros((CONV_CARRY_ROWS, x_ref.shape[2]), F32)
        om_ref[...], meta_carry_ref[...] = _conv_rows(m_ref[...], zeros, *weights)

    @pl.when(pl.program_id(1) == 0)
    def _():
        carry_ref[...] = meta_carry_ref[...]

    prev = carry_ref[...]
    for s in range(CONV_SUBTILES):
        rows = slice(s * CONV_SUB_TILE, (s + 1) * CONV_SUB_TILE)
        ox_ref[0, rows, :], prev = _conv_rows(x_ref[0, rows, :], prev, *weights)
    carry_ref[...] = prev


def _conv_block(hx, hm, w_in, w_conv, w_out, gain, bias):
    bsz, seq, d = hx.shape
    tm = CONV_SUB_TILE * CONV_SUBTILES
    assert seq % tm == 0 and CONV_WIDTH - 1 <= CONV_CARRY_ROWS <= N_META
    x_spec = pl.BlockSpec((1, tm, d), lambda b, t: (b, t, 0))
    return pl.pallas_call(
        _conv_kernel,
        grid=(bsz, seq // tm),
        in_specs=[x_spec, _resident((N_META, d)), _resident((d, 3 * d)), _resident((CONV_WIDTH, d)),
                  _resident((d, d)), _resident((1, d)), _resident((1, d))],
        out_specs=(x_spec, pl.BlockSpec((N_META, d), lambda b, t: (0, 0))),
        out_shape=(jax.ShapeDtypeStruct((bsz, seq, d), F32), jax.ShapeDtypeStruct((N_META, d), F32)),
        scratch_shapes=[pltpu.VMEM((CONV_CARRY_ROWS, d), F32), pltpu.VMEM((CONV_CARRY_ROWS, d), F32)],
        compiler_params=pltpu.CompilerParams(
            dimension_semantics=("arbitrary", "arbitrary"), vmem_limit_bytes=VMEM_LIMIT_BYTES),
        name="conv_block",
    )(hx, hm, w_in, w_conv, w_out, gain, bias)


def _split3_bf16(a):
    hi = a.astype(BF16)
    r = a - hi.astype(F32)
    mid = r.astype(BF16)
    lo = (r - mid.astype(F32)).astype(BF16)
    return hi, mid, lo


def _gla_rows(x, c, state_ref, weights, scratch):
    (w_in_ref, w_gu_ref, b_gate_ref, norm_w_ref, w_out_ref, gain_ref, bias_ref) = weights
    (q_ref, k_ref, v_ref, la_ref, b_ref, og_ref, qin_ref, qt_ref, kt_ref, kst_ref, kstT_ref,
     p_ref, upd_ref, gate_ref) = scratch
    n = x.shape[0]
    n_chunks = n // c
    all_rows = slice(0, n)
    o_k, o_v, o_g, o_gz = GLA_DK, 2 * GLA_DK, 2 * GLA_DK + GLA_DV, 2 * GLA_DK + 2 * GLA_DV
    xb = x.astype(BF16)
    gz = _dot(xb, w_in_ref[:, o_gz:])
    q_ref[all_rows, :] = _dot(xb, w_in_ref[:, :o_k]) * (GLA_HEAD_K ** -0.5)
    z = _dot(gz.astype(BF16), w_gu_ref[...]) + b_gate_ref[...]
    k_ref[all_rows, :] = _dot(xb, w_in_ref[:, o_k:o_v])
    v_ref[all_rows, :] = _dot(xb, w_in_ref[:, o_v:o_g]).astype(BF16)
    la_ref[all_rows, :] = (jnp.minimum(z, 0.0) - jnp.log1p(jnp.exp(-jnp.abs(z)))) * (1.0 / GLA_GATE_NORMALIZER)

    ri = lax.broadcasted_iota(jnp.int32, (c, c), 0)
    ci = lax.broadcasted_iota(jnp.int32, (c, c), 1)
    causal = ci <= ri
    tril = causal.astype(BF16)
    head_k = [slice(hd * GLA_HEAD_K, (hd + 1) * GLA_HEAD_K) for hd in range(GLA_HEADS)]
    head_v = [slice(hd * GLA_HEAD_V, (hd + 1) * GLA_HEAD_V) for hd in range(GLA_HEADS)]
    chunk_rows = [slice(ic * c, (ic + 1) * c) for ic in range(n_chunks)]

    for rows in chunk_rows:
        hi, mid, lo = _split3_bf16(la_ref[rows, :])
        b_ref[rows, :] = _dot(tril, hi) + _dot(tril, mid) + _dot(tril, lo)
    gate = _dot(xb, w_in_ref[:, o_g:o_gz])
    gate_ref[all_rows, :] = gate * jax.nn.sigmoid(gate)
    decays = []
    for rows in chunk_rows:
        b = b_ref[rows, :]
        b_mid = b[c // 2:c // 2 + 1]
        b_last = b[c - 1:c]
        q_t = q_ref[rows, :] * jnp.exp(b - b_mid)
        k_t = k_ref[rows, :] * jnp.exp(b_mid - b)
        qin_ref[rows, :] = (q_t * jnp.exp(b_mid)).astype(BF16)
        kst_ref[rows, :] = (k_t * jnp.exp(b_last - b_mid)).astype(BF16)
        qt_ref[rows, :] = q_t.astype(BF16)
        kt_ref[rows, :] = k_t.astype(BF16)
        decays.append(jnp.exp(b_last))
    for ic, rows in enumerate(chunk_rows):
        kstT_ref[ic, :, 0:c] = jnp.transpose(kst_ref[rows, :])
    for ic, rows in enumerate(chunk_rows):
        for hd in range(GLA_HEADS):
            scores = lax.dot_general(qt_ref[rows, head_k[hd]], kt_ref[rows, head_k[hd]],
                                     (((1,), (1,)), ((), ())), preferred_element_type=F32)
            p_ref[hd, rows, 0:c] = jnp.where(causal, scores, 0.0).astype(BF16)
            upd_ref[ic, hd] = _dot(kstT_ref[ic, head_k[hd], 0:c], v_ref[rows, head_v[hd]])

    pad = [jnp.zeros_like(decays[0])] * (-n_chunks % 8)
    dec_rows = jnp.concatenate(decays + pad, axis=0)
    dec_cols = [jnp.transpose(dec_rows[:, head_k[hd]]) for hd in range(GLA_HEADS)]

    norm_w = norm_w_ref[...]
    for ic, rows in enumerate(chunk_rows):
        for hd in range(GLA_HEADS):
            s_prev = state_ref[hd]
            o = (_dot(p_ref[hd, rows, 0:c], v_ref[rows, head_v[hd]])
                 + _dot(qin_ref[rows, head_k[hd]], s_prev.astype(BF16)))
            state_ref[hd] = dec_cols[hd][:, ic:ic + 1] * s_prev + upd_ref[ic, hd]
            o = o * lax.rsqrt(jnp.mean(o * o, axis=-1, keepdims=True) + RMS_EPS) * norm_w
            og_ref[rows, head_v[hd]] = (o * gate_ref[rows, head_v[hd]]).astype(BF16)

    halves = (slice(0, n // 2), slice(n // 2, n)) if n >= 2 * GLA_OUT_MIN_ROWS else (all_rows,)
    outs = []
    for r in halves:
        mix = _dot(og_ref[r, :], w_out_ref[...])
        outs.append(_layer_norm(DEEPNORM_ALPHA * x[r] + mix, gain_ref[...], bias_ref[...]))
    return jnp.concatenate(outs, axis=0)


def _gla_kernel(x_ref, m_ref, w_in_ref, w_gu_ref, b_gate_ref, norm_w_ref, w_out_ref, gain_ref, bias_ref,
                ox_ref, om_ref, state_ref, meta_state_ref, *scratch):
    weights = (w_in_ref, w_gu_ref, b_gate_ref, norm_w_ref, w_out_ref, gain_ref, bias_ref)

    @pl.when((pl.program_id(0) == 0) & (pl.program_id(1) == 0))
    def _():
        meta_state_ref[...] = jnp.zeros_like(meta_state_ref)
        om_ref[...] = _gla_rows(m_ref[...], N_META, meta_state_ref, weights, scratch)

    @pl.when(pl.program_id(1) == 0)
    def _():
        state_ref[...] = meta_state_ref[...]

    ox_ref[0] = _gla_rows(x_ref[0], GLA_CHUNK, state_ref, weights, scratch)


def _gla_block(hx, hm, w_in, w_gu, b_gate, norm_w, w_out, gain, bias):
    bsz, seq, d = hx.shape
    tm = GLA_TILE
    assert seq % tm == 0 and tm % GLA_CHUNK == 0 and N_META <= GLA_CHUNK
    n_chunks = tm // GLA_CHUNK
    x_spec = pl.BlockSpec((1, tm, d), lambda b, t: (b, t, 0))
    state = pltpu.VMEM((GLA_HEADS, GLA_HEAD_K, GLA_HEAD_V), F32)
    return pl.pallas_call(
        _gla_kernel,
        grid=(bsz, seq // tm),
        in_specs=[x_spec, _resident((N_META, d)), _resident(w_in.shape), _resident((GLA_GATE_RANK, GLA_DK)),
                  _resident((1, GLA_DK)), _resident((1, GLA_HEAD_V)), _resident((GLA_DV, d)),
                  _resident((1, d)), _resident((1, d))],
        out_specs=(x_spec, pl.BlockSpec((N_META, d), lambda b, t: (0, 0))),
        out_shape=(jax.ShapeDtypeStruct((bsz, seq, d), F32), jax.ShapeDtypeStruct((N_META, d), F32)),
        scratch_shapes=[
            state, state,
            pltpu.VMEM((tm, GLA_DK), F32),
            pltpu.VMEM((tm, GLA_DK), F32),
            pltpu.VMEM((tm, GLA_DV), BF16),
            pltpu.VMEM((tm, GLA_DK), F32),
            pltpu.VMEM((tm, GLA_DK), F32),
            pltpu.VMEM((tm, GLA_DV), BF16),
            pltpu.VMEM((tm, GLA_DK), BF16),
            pltpu.VMEM((tm, GLA_DK), BF16),
            pltpu.VMEM((tm, GLA_DK), BF16),
            pltpu.VMEM((tm, GLA_DK), BF16),
            pltpu.VMEM((n_chunks, GLA_DK, GLA_CHUNK), BF16),
            pltpu.VMEM((GLA_HEADS, tm, GLA_CHUNK), BF16),
            pltpu.VMEM((n_chunks, GLA_HEADS, GLA_HEAD_K, GLA_HEAD_V), F32),
            pltpu.VMEM((tm, GLA_DV), F32),
        ],
        compiler_params=pltpu.CompilerParams(
            dimension_semantics=("arbitrary", "arbitrary"), vmem_limit_bytes=VMEM_LIMIT_BYTES),
        name="gla_block",
    )(hx, hm, w_in, w_gu, b_gate, norm_w, w_out, gain, bias)


def kernel(x, meta_tokens, ln_gain, ln_bias, ffn_w_in, ffn_w_out, conv_w_in, conv_w, conv_w_out,
           gla_w_in, gla_w_gate_up, gla_b_gate, gla_norm_w, gla_w_out):
    bsz, seq, d = x.shape
    hx = x
    hm = meta_tokens.astype(x.dtype)

    ffn_w_in_bf16 = ffn_w_in.astype(BF16)
    ffn_w_out_bf16 = ffn_w_out.astype(BF16)

    def ffn(hx, hm, i, half):
        ox, om = _ffn_block(hx.reshape(bsz * seq, d), hm, ffn_w_in_bf16, ffn_w_out_bf16,
                            ln_gain[i, 2 * half][None], ln_bias[i, 2 * half][None], (i, half))
        return ox.reshape(bsz, seq, d), om

    for i in range(DEPTH):
        hx, hm = ffn(hx, hm, i, 0)
        j = i // 2
        gain, bias = ln_gain[i, 1][None], ln_bias[i, 1][None]
        if i % 2 == 0:
            hx, hm = _conv_block(hx, hm, conv_w_in[j].astype(BF16), conv_w[j], conv_w_out[j].astype(BF16),
                                 gain, bias)
        else:
            hx, hm = _gla_block(hx, hm, gla_w_in[j].astype(BF16), gla_w_gate_up[j].astype(BF16),
                                gla_b_gate[j][None], gla_norm_w[j][None], gla_w_out[j].astype(BF16), gain, bias)
        hx, hm = ffn(hx, hm if i + 1 < DEPTH else None, i, 1)
    return hx
```

```python
import jax
import jax.numpy as jnp
from jax import lax
from jax.experimental import pallas as pl
from jax.experimental.pallas import tpu as pltpu

D_MODEL = 1024
DEPTH = 2
N_META = 16
CONV_WIDTH = 3
GLA_HEADS = 4
GLA_DK = D_MODEL // 2
GLA_DV = D_MODEL
GLA_HEAD_K = GLA_DK // GLA_HEADS
GLA_HEAD_V = GLA_DV // GLA_HEADS
GLA_GATE_RANK = 16
GLA_GATE_NORMALIZER = 16.0
GLA_CHUNK = 64
DEEPNORM_ALPHA = (2.0 * DEPTH) ** 0.25
LN_EPS = 1e-5
RMS_EPS = 1e-6

CONV_CARRY_ROWS = 8
FFN_TILE = 512
FFN_STAGE_COLS = 1408
CONV_SUB_TILE = 256
CONV_SUBTILES = 4
GLA_TILE = 512
GLA_OUT_MIN_ROWS = 256
VMEM_LIMIT_BYTES = 56 * 1024 * 1024

F32 = jnp.float32
BF16 = jnp.bfloat16


def _dot(a, b):
    return jnp.dot(a, b, preferred_element_type=F32)


def _layer_norm(y, gain, bias):
    mu = jnp.mean(y, axis=-1, keepdims=True)
    yc = y - mu
    var = jnp.mean(yc * yc, axis=-1, keepdims=True)
    return yc * lax.rsqrt(var + LN_EPS) * gain + bias


def _resident(shape):
    zeros = (0,) * len(shape)
    return pl.BlockSpec(shape, lambda *_: zeros, pipeline_mode=pl.Buffered(1))


def _ffn_residual(x, w_in_ref, w_out_ref, before_gate=None, before_up=None, before_out=None):
    f = w_out_ref.shape[0]
    xb = x.astype(BF16)
    if before_gate is not None:
        before_gate()
    gate = _dot(xb, w_in_ref[:, :f])
    if before_up is not None:
        before_up()
    up = _dot(xb, w_in_ref[:, f:])
    act = (gate * jax.nn.sigmoid(gate) * up).astype(BF16)
    if before_out is not None:
        before_out()
    return DEEPNORM_ALPHA * x + 0.5 * _dot(act, w_out_ref[...])


def _ffn_kernel(with_meta, n_tiles, sel, *refs):
    if with_meta:
        (x_ref, m_ref, w_in_hbm, w_out_hbm, gain_ref, bias_ref, ox_ref, om_ref,
         y_ref, w_in_ref, w_out_ref, stage_ref, sem) = refs
    else:
        (x_ref, w_in_hbm, w_out_hbm, gain_ref, bias_ref, ox_ref,
         y_ref, w_in_ref, w_out_ref, stage_ref, sem) = refs
    i = pl.program_id(0)
    d, cols = stage_ref.shape[1], stage_ref.shape[2]
    f = w_out_ref.shape[0]
    layer, half = sel
    pieces = []
    for c0 in range(0, 2 * f, cols):
        pieces.append((w_in_hbm.at[layer, half, :, pl.ds(c0, cols)], (slice(0, d), slice(0, cols)),
                       w_in_ref.at[:, pl.ds(c0, cols)]))
    for r0 in range(0, f, d):
        rows = min(d, f - r0)
        pieces.append((w_out_hbm.at[layer, half, pl.ds(r0, rows), :], (slice(0, rows), slice(0, d)),
                       w_out_ref.at[pl.ds(r0, rows), :]))
    n_in = 2 * f // cols

    def copy(k):
        src, view, _ = pieces[k]
        return pltpu.make_async_copy(src, stage_ref.at[k % 2, view[0], view[1]], sem.at[k])

    def land(k):
        _, view, dst = pieces[k]
        copy(k).wait()
        dst[...] = stage_ref[k % 2, view[0], view[1]].astype(BF16)
        if k + 2 < len(pieces):
            copy(k + 2).start()

    def land_range(lo, hi):
        def run():
            for k in range(lo, hi):
                land(k)
        return run

    def normalise_previous():
        ox_ref[...] = _layer_norm(y_ref[...], gain_ref[...], bias_ref[...])

    def form_current(*hooks):
        y_ref[...] = _ffn_residual(x_ref[...], w_in_ref, w_out_ref, *hooks)

    @pl.when(i == 0)
    def _():
        copy(0).start()
        copy(1).start()
        form_current(land_range(0, n_in // 2), land_range(n_in // 2, n_in), land_range(n_in, len(pieces)))
        if with_meta:
            om_ref[...] = _layer_norm(_ffn_residual(m_ref[...], w_in_ref, w_out_ref),
                                      gain_ref[...], bias_ref[...])

    @pl.when((i > 0) & (i < n_tiles))
    def _():
        normalise_previous()
        form_current()

    @pl.when(i == n_tiles)
    def _():
        normalise_previous()


def _ffn_block(hx, hm, w_in, w_out, gain, bias, sel):
    n, d = hx.shape
    f = w_out.shape[2]
    tm = FFN_TILE
    assert n % tm == 0 and (2 * f) % FFN_STAGE_COLS == 0 and f % FFN_STAGE_COLS == 0 and d <= FFN_STAGE_COLS
    n_tiles = n // tm
    n_pieces = 2 * f // FFN_STAGE_COLS + pl.cdiv(f, d)
    with_meta = hm is not None
    x_spec = pl.BlockSpec((tm, d), lambda i: (jnp.minimum(i, n_tiles - 1), 0))
    o_spec = pl.BlockSpec((tm, d), lambda i: (jnp.maximum(i - 1, 0), 0))
    hbm = pl.BlockSpec(memory_space=pl.ANY)
    w_specs = [hbm, hbm, _resident((1, d)), _resident((1, d))]
    params = pltpu.CompilerParams(dimension_semantics=("arbitrary",), vmem_limit_bytes=VMEM_LIMIT_BYTES)
    scratch = [pltpu.VMEM((tm, d), F32),
               pltpu.VMEM((d, 2 * f), BF16), pltpu.VMEM((f, d), BF16),
               pltpu.VMEM((2, d, FFN_STAGE_COLS), F32),
               pltpu.SemaphoreType.DMA((n_pieces,))]
    if not with_meta:
        return pl.pallas_call(
            lambda *refs: _ffn_kernel(False, n_tiles, sel, *refs),
            grid=(n_tiles + 1,), in_specs=[x_spec] + w_specs, out_specs=o_spec,
            out_shape=jax.ShapeDtypeStruct((n, d), F32), scratch_shapes=scratch,
            compiler_params=params, name="ffn_block",
        )(hx, w_in, w_out, gain, bias), None
    return pl.pallas_call(
        lambda *refs: _ffn_kernel(True, n_tiles, sel, *refs),
        grid=(n_tiles + 1,), in_specs=[x_spec, _resident((N_META, d))] + w_specs,
        out_specs=(o_spec, pl.BlockSpec((N_META, d), lambda i: (0, 0))),
        out_shape=(jax.ShapeDtypeStruct((n, d), F32), jax.ShapeDtypeStruct((N_META, d), F32)),
        scratch_shapes=scratch, compiler_params=params, name="ffn_meta_block",
    )(hx, hm, w_in, w_out, gain, bias)


def _conv_rows(x, prev, w_in_ref, w_conv_ref, w_out_ref, gain_ref, bias_ref):
    n, d = x.shape
    xb = x.astype(BF16)
    u = _dot(xb, w_in_ref[:, d:2 * d]) * _dot(xb, w_in_ref[:, 2 * d:])
    bgate = _dot(xb, w_in_ref[:, :d])
    row = lax.broadcasted_iota(jnp.int32, (n, 1), 0)
    last = prev[CONV_CARRY_ROWS - 1:CONV_CARRY_ROWS]
    u1 = jnp.where(row == 0, last, pltpu.roll(u, 1, 0))
    u2 = jnp.where(row == 0, prev[CONV_CARRY_ROWS - 2:CONV_CARRY_ROWS - 1],
                   jnp.where(row == 1, last, pltpu.roll(u, 2, 0)))
    w = w_conv_ref[...]
    conv = w[0:1] * u2 + w[1:2] * u1 + w[2:3] * u
    mix = _dot((bgate * conv).astype(BF16), w_out_ref[...])
    out = _layer_norm(DEEPNORM_ALPHA * x + mix, gain_ref[...], bias_ref[...])
    return out, u[n - CONV_CARRY_ROWS:]


def _conv_kernel(x_ref, m_ref, w_in_ref, w_conv_ref, w_out_ref, gain_ref, bias_ref,
                 ox_ref, om_ref, carry_ref, meta_carry_ref):
    weights = (w_in_ref, w_conv_ref, w_out_ref, gain_ref, bias_ref)

    @pl.when((pl.program_id(0) == 0) & (pl.program_id(1) == 0))
    def _():
        zeros = jnp.zeros((CONV_CARRY_ROWS, x_ref.shape[2]), F32)
        om_ref[...], meta_carry_ref[...] = _conv_rows(m_ref[...], zeros, *weights)

    @pl.when(pl.program_id(1) == 0)
    def _():
        carry_ref[...] = meta_carry_ref[...]

    prev = carry_ref[...]
    for s in range(CONV_SUBTILES):
        rows = slice(s * CONV_SUB_TILE, (s + 1) * CONV_SUB_TILE)
        ox_ref[0, rows, :], prev = _conv_rows(x_ref[0, rows, :], prev, *weights)
    carry_ref[...] = prev


def _conv_block(hx, hm, w_in, w_conv, w_out, gain, bias):
    bsz, seq, d = hx.shape
    tm = CONV_SUB_TILE * CONV_SUBTILES
    assert seq % tm == 0 and CONV_WIDTH - 1 <= CONV_CARRY_ROWS <= N_META
    x_spec = pl.BlockSpec((1, tm, d), lambda b, t: (b, t, 0))
    return pl.pallas_call(
        _conv_kernel,
        grid=(bsz, seq // tm),
        in_specs=[x_spec, _resident((N_META, d)), _resident((d, 3 * d)), _resident((CONV_WIDTH, d)),
                  _resident((d, d)), _resident((1, d)), _resident((1, d))],
        out_specs=(x_spec, pl.BlockSpec((N_META, d), lambda b, t: (0, 0))),
        out_shape=(jax.ShapeDtypeStruct((bsz, seq, d), F32), jax.ShapeDtypeStruct((N_META, d), F32)),
        scratch_shapes=[pltpu.VMEM((CONV_CARRY_ROWS, d), F32), pltpu.VMEM((CONV_CARRY_ROWS, d), F32)],
        compiler_params=pltpu.CompilerParams(
            dimension_semantics=("arbitrary", "arbitrary"), vmem_limit_bytes=VMEM_LIMIT_BYTES),
        name="conv_block",
    )(hx, hm, w_in, w_conv, w_out, gain, bias)


def _split3_bf16(a):
    hi = a.astype(BF16)
    r = a - hi.astype(F32)
    mid = r.astype(BF16)
    lo = (r - mid.astype(F32)).astype(BF16)
    return hi, mid, lo


def _gla_rows(x, c, state_ref, weights, scratch):
    (w_in_ref, w_gu_ref, b_gate_ref, norm_w_ref, w_out_ref, gain_ref, bias_ref) = weights
    (q_ref, k_ref, v_ref, la_ref, b_ref, og_ref, qin_ref, qt_ref, kt_ref, kst_ref, kstT_ref,
     p_ref, upd_ref, gate_ref) = scratch
    n = x.shape[0]
    n_chunks = n // c
    all_rows = slice(0, n)
    o_k, o_v, o_g, o_gz = GLA_DK, 2 * GLA_DK, 2 * GLA_DK + GLA_DV, 2 * GLA_DK + 2 * GLA_DV
    xb = x.astype(BF16)
    gz = _dot(xb, w_in_ref[:, o_gz:])
    q_ref[all_rows, :] = _dot(xb, w_in_ref[:, :o_k]) * (GLA_HEAD_K ** -0.5)
    z = _dot(gz.astype(BF16), w_gu_ref[...]) + b_gate_ref[...]
    k_ref[all_rows, :] = _dot(xb, w_in_ref[:, o_k:o_v])
    v_ref[all_rows, :] = _dot(xb, w_in_ref[:, o_v:o_g]).astype(BF16)
    la_ref[all_rows, :] = (jnp.minimum(z, 0.0) - jnp.log1p(jnp.exp(-jnp.abs(z)))) * (1.0 / GLA_GATE_NORMALIZER)

    ri = lax.broadcasted_iota(jnp.int32, (c, c), 0)
    ci = lax.broadcasted_iota(jnp.int32, (c, c), 1)
    causal = ci <= ri
    tril = causal.astype(BF16)
    head_k = [slice(hd * GLA_HEAD_K, (hd + 1) * GLA_HEAD_K) for hd in range(GLA_HEADS)]
    head_v = [slice(hd * GLA_HEAD_V, (hd + 1) * GLA_HEAD_V) for hd in range(GLA_HEADS)]
    chunk_rows = [slice(ic * c, (ic + 1) * c) for ic in range(n_chunks)]

    for rows in chunk_rows:
        hi, mid, lo = _split3_bf16(la_ref[rows, :])
        b_ref[rows, :] = _dot(tril, hi) + _dot(tril, mid) + _dot(tril, lo)
    gate = _dot(xb, w_in_ref[:, o_g:o_gz])
    gate_ref[all_rows, :] = gate * jax.nn.sigmoid(gate)
    decays = []
    for rows in chunk_rows:
        b = b_ref[rows, :]
        b_mid = b[c // 2:c // 2 + 1]
        b_last = b[c - 1:c]
        q_t = q_ref[rows, :] * jnp.exp(b - b_mid)
        k_t = k_ref[rows, :] * jnp.exp(b_mid - b)
        qin_ref[rows, :] = (q_t * jnp.exp(b_mid)).astype(BF16)
        kst_ref[rows, :] = (k_t * jnp.exp(b_last - b_mid)).astype(BF16)
        qt_ref[rows, :] = q_t.astype(BF16)
        kt_ref[rows, :] = k_t.astype(BF16)
        decays.append(jnp.exp(b_last))
    for ic, rows in enumerate(chunk_rows):
        kstT_ref[ic, :, 0:c] = jnp.transpose(kst_ref[rows, :])
    for ic, rows in enumerate(chunk_rows):
        for hd in range(GLA_HEADS):
            scores = lax.dot_general(qt_ref[rows, head_k[hd]], kt_ref[rows, head_k[hd]],
                                     (((1,), (1,)), ((), ())), preferred_element_type=F32)
            p_ref[hd, rows, 0:c] = jnp.where(causal, scores, 0.0).astype(BF16)
            upd_ref[ic, hd] = _dot(kstT_ref[ic, head_k[hd], 0:c], v_ref[rows, head_v[hd]])

    pad = [jnp.zeros_like(decays[0])] * (-n_chunks % 8)
    dec_rows = jnp.concatenate(decays + pad, axis=0)
    dec_cols = [jnp.transpose(dec_rows[:, head_k[hd]]) for hd in range(GLA_HEADS)]

    norm_w = norm_w_ref[...]
    for ic, rows in enumerate(chunk_rows):
        for hd in range(GLA_HEADS):
            s_prev = state_ref[hd]
            o = (_dot(p_ref[hd, rows, 0:c], v_ref[rows, head_v[hd]])
                 + _dot(qin_ref[rows, head_k[hd]], s_prev.astype(BF16)))
            state_ref[hd] = dec_cols[hd][:, ic:ic + 1] * s_prev + upd_ref[ic, hd]
            o = o * lax.rsqrt(jnp.mean(o * o, axis=-1, keepdims=True) + RMS_EPS) * norm_w
            og_ref[rows, head_v[hd]] = (o * gate_ref[rows, head_v[hd]]).astype(BF16)

    halves = (slice(0, n // 2), slice(n // 2, n)) if n >= 2 * GLA_OUT_MIN_ROWS else (all_rows,)
    outs = []
    for r in halves:
        mix = _dot(og_ref[r, :], w_out_ref[...])
        outs.append(_layer_norm(DEEPNORM_ALPHA * x[r] + mix, gain_ref[...], bias_ref[...]))
    return jnp.concatenate(outs, axis=0)


def _gla_kernel(x_ref, m_ref, w_in_ref, w_gu_ref, b_gate_ref, norm_w_ref, w_out_ref, gain_ref, bias_ref,
                ox_ref, om_ref, state_ref, meta_state_ref, *scratch):
    weights = (w_in_ref, w_gu_ref, b_gate_ref, norm_w_ref, w_out_ref, gain_ref, bias_ref)

    @pl.when((pl.program_id(0) == 0) & (pl.program_id(1) == 0))
    def _():
        meta_state_ref[...] = jnp.zeros_like(meta_state_ref)
        om_ref[...] = _gla_rows(m_ref[...], N_META, meta_state_ref, weights, scratch)

    @pl.when(pl.program_id(1) == 0)
    def _():
        state_ref[...] = meta_state_ref[...]

    ox_ref[0] = _gla_rows(x_ref[0], GLA_CHUNK, state_ref, weights, scratch)


def _gla_block(hx, hm, w_in, w_gu, b_gate, norm_w, w_out, gain, bias):
    bsz, seq, d = hx.shape
    tm = GLA_TILE
    assert seq % tm == 0 and tm % GLA_CHUNK == 0 and N_META <= GLA_CHUNK
    n_chunks = tm // GLA_CHUNK
    x_spec = pl.BlockSpec((1, tm, d), lambda b, t: (b, t, 0))
    state = pltpu.VMEM((GLA_HEADS, GLA_HEAD_K, GLA_HEAD_V), F32)
    return pl.pallas_call(
        _gla_kernel,
        grid=(bsz, seq // tm),
        in_specs=[x_spec, _resident((N_META, d)), _resident(w_in.shape), _resident((GLA_GATE_RANK, GLA_DK)),
                  _resident((1, GLA_DK)), _resident((1, GLA_HEAD_V)), _resident((GLA_DV, d)),
                  _resident((1, d)), _resident((1, d))],
        out_specs=(x_spec, pl.BlockSpec((N_META, d), lambda b, t: (0, 0))),
        out_shape=(jax.ShapeDtypeStruct((bsz, seq, d), F32), jax.ShapeDtypeStruct((N_META, d), F32)),
        scratch_shapes=[
            state, state,
            pltpu.VMEM((tm, GLA_DK), F32),
            pltpu.VMEM((tm, GLA_DK), F32),
            pltpu.VMEM((tm, GLA_DV), BF16),
            pltpu.VMEM((tm, GLA_DK), F32),
            pltpu.VMEM((tm, GLA_DK), F32),
            pltpu.VMEM((tm, GLA_DV), BF16),
            pltpu.VMEM((tm, GLA_DK), BF16),
            pltpu.VMEM((tm, GLA_DK), BF16),
            pltpu.VMEM((tm, GLA_DK), BF16),
            pltpu.VMEM((tm, GLA_DK), BF16),
            pltpu.VMEM((n_chunks, GLA_DK, GLA_CHUNK), BF16),
            pltpu.VMEM((GLA_HEADS, tm, GLA_CHUNK), BF16),
            pltpu.VMEM((n_chunks, GLA_HEADS, GLA_HEAD_K, GLA_HEAD_V), F32),
            pltpu.VMEM((tm, GLA_DV), F32),
        ],
        compiler_params=pltpu.CompilerParams(
            dimension_semantics=("arbitrary", "arbitrary"), vmem_limit_bytes=VMEM_LIMIT_BYTES),
        name="gla_block",
    )(hx, hm, w_in, w_gu, b_gate, norm_w, w_out, gain, bias)


def kernel(x, meta_tokens, ln_gain, ln_bias, ffn_w_in, ffn_w_out, conv_w_in, conv_w, conv_w_out,
           gla_w_in, gla_w_gate_up, gla_b_gate, gla_norm_w, gla_w_out):
    bsz, seq, d = x.shape
    hx = x
    hm = meta_tokens.astype(x.dtype)

    def ffn(hx, hm, i, half):
        ox, om = _ffn_block(hx.reshape(bsz * seq, d), hm, ffn_w_in, ffn_w_out,
                            ln_gain[i, 2 * half][None], ln_bias[i, 2 * half][None], (i, half))
        return ox.reshape(bsz, seq, d), om

    for i in range(DEPTH):
        hx, hm = ffn(hx, hm, i, 0)
        j = i // 2
        gain, bias = ln_gain[i, 1][None], ln_bias[i, 1][None]
        if i % 2 == 0:
            hx, hm = _conv_block(hx, hm, conv_w_in[j].astype(BF16), conv_w[j], conv_w_out[j].astype(BF16),
                                 gain, bias)
        else:
            hx, hm = _gla_block(hx, hm, gla_w_in[j].astype(BF16), gla_w_gate_up[j].astype(BF16),
                                gla_b_gate[j][None], gla_norm_w[j][None], gla_w_out[j].astype(BF16), gain, bias)
        hx, hm = ffn(hx, hm if i + 1 < DEPTH else None, i, 1)
    return hx
```

```python
import jax
import jax.numpy as jnp
from jax import lax
from jax.experimental import pallas as pl
from jax.experimental.pallas import tpu as pltpu

D_MODEL = 1024
DEPTH = 2
N_META = 16
CONV_WIDTH = 3
GLA_HEADS = 4
GLA_DK = D_MODEL // 2
GLA_DV = D_MODEL
GLA_HEAD_K = GLA_DK // GLA_HEADS
GLA_HEAD_V = GLA_DV // GLA_HEADS
GLA_GATE_RANK = 16
GLA_GATE_NORMALIZER = 16.0
GLA_CHUNK = 64
DEEPNORM_ALPHA = (2.0 * DEPTH) ** 0.25
LN_EPS = 1e-5
RMS_EPS = 1e-6

CONV_CARRY_ROWS = 8
FFN_TILE = 512
FFN_STAGE_COLS = 1408
FFN_STAGE_SLOTS = 3
CONV_SUB_TILE = 256
CONV_SUBTILES = 4
GLA_TILE = 512
GLA_OUT_MIN_ROWS = 256
VMEM_LIMIT_BYTES = 56 * 1024 * 1024

F32 = jnp.float32
BF16 = jnp.bfloat16


def _dot(a, b):
    return jnp.dot(a, b, preferred_element_type=F32)


def _layer_norm(y, gain, bias):
    mu = jnp.mean(y, axis=-1, keepdims=True)
    yc = y - mu
    var = jnp.mean(yc * yc, axis=-1, keepdims=True)
    return yc * lax.rsqrt(var + LN_EPS) * gain + bias


def _resident(shape):
    zeros = (0,) * len(shape)
    return pl.BlockSpec(shape, lambda *_: zeros, pipeline_mode=pl.Buffered(1))


def _ffn_residual(x, w_in_ref, w_out_ref, before_gate=None, before_up=None, before_out=None):
    f = w_out_ref.shape[0]
    xb = x.astype(BF16)
    if before_gate is not None:
        before_gate()
    gate = _dot(xb, w_in_ref[:, :f])
    if before_up is not None:
        before_up()
    up = _dot(xb, w_in_ref[:, f:])
    act = (gate * jax.nn.sigmoid(gate) * up).astype(BF16)
    if before_out is not None:
        before_out()
    return DEEPNORM_ALPHA * x + 0.5 * _dot(act, w_out_ref[...])


def _ffn_kernel(with_meta, n_tiles, sel, *refs):
    if with_meta:
        (x_ref, m_ref, w_in_hbm, w_out_hbm, gain_ref, bias_ref, ox_ref, om_ref,
         y_ref, w_in_ref, w_out_ref, stage_ref, sem) = refs
    else:
        (x_ref, w_in_hbm, w_out_hbm, gain_ref, bias_ref, ox_ref,
         y_ref, w_in_ref, w_out_ref, stage_ref, sem) = refs
    i = pl.program_id(0)
    d, cols = stage_ref.shape[1], stage_ref.shape[2]
    f = w_out_ref.shape[0]
    layer, half = sel
    pieces = []
    for c0 in range(0, 2 * f, cols):
        pieces.append((w_in_hbm.at[layer, half, :, pl.ds(c0, cols)], (slice(0, d), slice(0, cols)),
                       w_in_ref.at[:, pl.ds(c0, cols)]))
    for r0 in range(0, f, d):
        rows = min(d, f - r0)
        pieces.append((w_out_hbm.at[layer, half, pl.ds(r0, rows), :], (slice(0, rows), slice(0, d)),
                       w_out_ref.at[pl.ds(r0, rows), :]))
    n_in = 2 * f // cols

    slots = stage_ref.shape[0]

    def copy(k):
        src, view, _ = pieces[k]
        return pltpu.make_async_copy(src, stage_ref.at[k % slots, view[0], view[1]], sem.at[k])

    def land(k):
        _, view, dst = pieces[k]
        copy(k).wait()
        dst[...] = stage_ref[k % slots, view[0], view[1]].astype(BF16)
        if k + slots < len(pieces):
            copy(k + slots).start()

    def land_range(lo, hi):
        def run():
            for k in range(lo, hi):
                land(k)
        return run

    def normalise_previous():
        ox_ref[...] = _layer_norm(y_ref[...], gain_ref[...], bias_ref[...])

    def form_current(*hooks):
        y_ref[...] = _ffn_residual(x_ref[...], w_in_ref, w_out_ref, *hooks)

    @pl.when(i == 0)
    def _():
        for k in range(slots):
            copy(k).start()
        form_current(land_range(0, n_in // 2), land_range(n_in // 2, n_in), land_range(n_in, len(pieces)))
        if with_meta:
            om_ref[...] = _layer_norm(_ffn_residual(m_ref[...], w_in_ref, w_out_ref),
                                      gain_ref[...], bias_ref[...])

    @pl.when((i > 0) & (i < n_tiles))
    def _():
        normalise_previous()
        form_current()

    @pl.when(i == n_tiles)
    def _():
        normalise_previous()


def _ffn_block(hx, hm, w_in, w_out, gain, bias, sel):
    n, d = hx.shape
    f = w_out.shape[2]
    tm = FFN_TILE
    assert n % tm == 0 and (2 * f) % FFN_STAGE_COLS == 0 and f % FFN_STAGE_COLS == 0 and d <= FFN_STAGE_COLS
    n_tiles = n // tm
    n_pieces = 2 * f // FFN_STAGE_COLS + pl.cdiv(f, d)
    with_meta = hm is not None
    x_spec = pl.BlockSpec((tm, d), lambda i: (jnp.minimum(i, n_tiles - 1), 0))
    o_spec = pl.BlockSpec((tm, d), lambda i: (jnp.maximum(i - 1, 0), 0))
    hbm = pl.BlockSpec(memory_space=pl.ANY)
    w_specs = [hbm, hbm, _resident((1, d)), _resident((1, d))]
    params = pltpu.CompilerParams(dimension_semantics=("arbitrary",), vmem_limit_bytes=VMEM_LIMIT_BYTES)
    scratch = [pltpu.VMEM((tm, d), F32),
               pltpu.VMEM((d, 2 * f), BF16), pltpu.VMEM((f, d), BF16),
               pltpu.VMEM((FFN_STAGE_SLOTS, d, FFN_STAGE_COLS), F32),
               pltpu.SemaphoreType.DMA((n_pieces,))]
    if not with_meta:
        return pl.pallas_call(
            lambda *refs: _ffn_kernel(False, n_tiles, sel, *refs),
            grid=(n_tiles + 1,), in_specs=[x_spec] + w_specs, out_specs=o_spec,
            out_shape=jax.ShapeDtypeStruct((n, d), F32), scratch_shapes=scratch,
            compiler_params=params, name="ffn_block",
        )(hx, w_in, w_out, gain, bias), None
    return pl.pallas_call(
        lambda *refs: _ffn_kernel(True, n_tiles, sel, *refs),
        grid=(n_tiles + 1,), in_specs=[x_spec, _resident((N_META, d))] + w_specs,
        out_specs=(o_spec, pl.BlockSpec((N_META, d), lambda i: (0, 0))),
        out_shape=(jax.ShapeDtypeStruct((n, d), F32), jax.ShapeDtypeStruct((N_META, d), F32)),
        scratch_shapes=scratch, compiler_params=params, name="ffn_meta_block",
    )(hx, hm, w_in, w_out, gain, bias)


def _conv_rows(x, prev, w_in_ref, w_conv_ref, w_out_ref, gain_ref, bias_ref):
    n, d = x.shape
    xb = x.astype(BF16)
    u = _dot(xb, w_in_ref[:, d:2 * d]) * _dot(xb, w_in_ref[:, 2 * d:])
    bgate = _dot(xb, w_in_ref[:, :d])
    row = lax.broadcasted_iota(jnp.int32, (n, 1), 0)
    last = prev[CONV_CARRY_ROWS - 1:CONV_CARRY_ROWS]
    u1 = jnp.where(row == 0, last, pltpu.roll(u, 1, 0))
    u2 = jnp.where(row == 0, prev[CONV_CARRY_ROWS - 2:CONV_CARRY_ROWS - 1],
                   jnp.where(row == 1, last, pltpu.roll(u, 2, 0)))
    w = w_conv_ref[...]
    conv = w[0:1] * u2 + w[1:2] * u1 + w[2:3] * u
    mix = _dot((bgate * conv).astype(BF16), w_out_ref[...])
    out = _layer_norm(DEEPNORM_ALPHA * x + mix, gain_ref[...], bias_ref[...])
    return out, u[n - CONV_CARRY_ROWS:]


def _conv_kernel(x_ref, m_ref, w_in_ref, w_conv_ref, w_out_ref, gain_ref, bias_ref,
                 ox_ref, om_ref, carry_ref, meta_carry_ref):
    weights = (w_in_ref, w_conv_ref, w_out_ref, gain_ref, bias_ref)

    @pl.when((pl.program_id(0) == 0) & (pl.program_id(1) == 0))
    def _():
        zeros = jnp.zeros((CONV_CARRY_ROWS, x_ref.shape[2]), F32)
        om_ref[...], meta_carry_ref[...] = _conv_rows(m_ref[...], zeros, *weights)

    @pl.when(pl.program_id(1) == 0)
    def _():
        carry_ref[...] = meta_carry_ref[...]

    prev = carry_ref[...]
    for s in range(CONV_SUBTILES):
        rows = slice(s * CONV_SUB_TILE, (s + 1) * CONV_SUB_TILE)
        ox_ref[0, rows, :], prev = _conv_rows(x_ref[0, rows, :], prev, *weights)
    carry_ref[...] = prev


def _conv_block(hx, hm, w_in, w_conv, w_out, gain, bias):
    bsz, seq, d = hx.shape
    tm = CONV_SUB_TILE * CONV_SUBTILES
    assert seq % tm == 0 and CONV_WIDTH - 1 <= CONV_CARRY_ROWS <= N_META
    x_spec = pl.BlockSpec((1, tm, d), lambda b, t: (b, t, 0))
    return pl.pallas_call(
        _conv_kernel,
        grid=(bsz, seq // tm),
        in_specs=[x_spec, _resident((N_META, d)), _resident((d, 3 * d)), _resident((CONV_WIDTH, d)),
                  _resident((d, d)), _resident((1, d)), _resident((1, d))],
        out_specs=(x_spec, pl.BlockSpec((N_META, d), lambda b, t: (0, 0))),
        out_shape=(jax.ShapeDtypeStruct((bsz, seq, d), F32), jax.ShapeDtypeStruct((N_META, d), F32)),
        scratch_shapes=[pltpu.VMEM((CONV_CARRY_ROWS, d), F32), pltpu.VMEM((CONV_CARRY_ROWS, d), F32)],
        compiler_params=pltpu.CompilerParams(
            dimension_semantics=("arbitrary", "arbitrary"), vmem_limit_bytes=VMEM_LIMIT_BYTES),
        name="conv_block",
    )(hx, hm, w_in, w_conv, w_out, gain, bias)


def _split3_bf16(a):
    hi = a.astype(BF16)
    r = a - hi.astype(F32)
    mid = r.astype(BF16)
    lo = (r - mid.astype(F32)).astype(BF16)
    return hi, mid, lo


def _gla_rows(x, c, state_ref, weights, scratch):
    (w_in_ref, w_gu_ref, b_gate_ref, norm_w_ref, w_out_ref, gain_ref, bias_ref) = weights
    (q_ref, k_ref, v_ref, la_ref, b_ref, og_ref, qin_ref, qt_ref, kt_ref, kst_ref, kstT_ref,
     p_ref, upd_ref, gate_ref) = scratch
    n = x.shape[0]
    n_chunks = n // c
    all_rows = slice(0, n)
    o_k, o_v, o_g, o_gz = GLA_DK, 2 * GLA_DK, 2 * GLA_DK + GLA_DV, 2 * GLA_DK + 2 * GLA_DV
    xb = x.astype(BF16)
    gz = _dot(xb, w_in_ref[:, o_gz:])
    q_ref[all_rows, :] = _dot(xb, w_in_ref[:, :o_k]) * (GLA_HEAD_K ** -0.5)
    z = _dot(gz.astype(BF16), w_gu_ref[...]) + b_gate_ref[...]
    k_ref[all_rows, :] = _dot(xb, w_in_ref[:, o_k:o_v])
    v_ref[all_rows, :] = _dot(xb, w_in_ref[:, o_v:o_g]).astype(BF16)
    la_ref[all_rows, :] = (jnp.minimum(z, 0.0) - jnp.log1p(jnp.exp(-jnp.abs(z)))) * (1.0 / GLA_GATE_NORMALIZER)

    ri = lax.broadcasted_iota(jnp.int32, (c, c), 0)
    ci = lax.broadcasted_iota(jnp.int32, (c, c), 1)
    causal = ci <= ri
    tril = causal.astype(BF16)
    head_k = [slice(hd * GLA_HEAD_K, (hd + 1) * GLA_HEAD_K) for hd in range(GLA_HEADS)]
    head_v = [slice(hd * GLA_HEAD_V, (hd + 1) * GLA_HEAD_V) for hd in range(GLA_HEADS)]
    chunk_rows = [slice(ic * c, (ic + 1) * c) for ic in range(n_chunks)]

    for rows in chunk_rows:
        hi, mid, lo = _split3_bf16(la_ref[rows, :])
        b_ref[rows, :] = _dot(tril, hi) + _dot(tril, mid) + _dot(tril, lo)
    gate = _dot(xb, w_in_ref[:, o_g:o_gz])
    gate_ref[all_rows, :] = gate * jax.nn.sigmoid(gate)
    decays = []
    for rows in chunk_rows:
        b = b_ref[rows, :]
        b_mid = b[c // 2:c // 2 + 1]
        b_last = b[c - 1:c]
        q_t = q_ref[rows, :] * jnp.exp(b - b_mid)
        k_t = k_ref[rows, :] * jnp.exp(b_mid - b)
        qin_ref[rows, :] = (q_t * jnp.exp(b_mid)).astype(BF16)
        kst_ref[rows, :] = (k_t * jnp.exp(b_last - b_mid)).astype(BF16)
        qt_ref[rows, :] = q_t.astype(BF16)
        kt_ref[rows, :] = k_t.astype(BF16)
        decays.append(jnp.exp(b_last))
    for ic, rows in enumerate(chunk_rows):
        kstT_ref[ic, :, 0:c] = jnp.transpose(kst_ref[rows, :])
    for ic, rows in enumerate(chunk_rows):
        for hd in range(GLA_HEADS):
            scores = lax.dot_general(qt_ref[rows, head_k[hd]], kt_ref[rows, head_k[hd]],
                                     (((1,), (1,)), ((), ())), preferred_element_type=F32)
            p_ref[hd, rows, 0:c] = jnp.where(causal, scores, 0.0).astype(BF16)
            upd_ref[ic, hd] = _dot(kstT_ref[ic, head_k[hd], 0:c], v_ref[rows, head_v[hd]])

    pad = [jnp.zeros_like(decays[0])] * (-n_chunks % 8)
    dec_rows = jnp.concatenate(decays + pad, axis=0)
    dec_cols = [jnp.transpose(dec_rows[:, head_k[hd]]) for hd in range(GLA_HEADS)]

    norm_w = norm_w_ref[...]
    for ic, rows in enumerate(chunk_rows):
        for hd in range(GLA_HEADS):
            s_prev = state_ref[hd]
            o = (_dot(p_ref[hd, rows, 0:c], v_ref[rows, head_v[hd]])
                 + _dot(qin_ref[rows, head_k[hd]], s_prev.astype(BF16)))
            state_ref[hd] = dec_cols[hd][:, ic:ic + 1] * s_prev + upd_ref[ic, hd]
            o = o * lax.rsqrt(jnp.mean(o * o, axis=-1, keepdims=True) + RMS_EPS) * norm_w
            og_ref[rows, head_v[hd]] = (o * gate_ref[rows, head_v[hd]]).astype(BF16)

    halves = (slice(0, n // 2), slice(n // 2, n)) if n >= 2 * GLA_OUT_MIN_ROWS else (all_rows,)
    outs = []
    for r in halves:
        mix = _dot(og_ref[r, :], w_out_ref[...])
        outs.append(_layer_norm(DEEPNORM_ALPHA * x[r] + mix, gain_ref[...], bias_ref[...]))
    return jnp.concatenate(outs, axis=0)


def _gla_kernel(x_ref, m_ref, w_in_ref, w_gu_ref, b_gate_ref, norm_w_ref, w_out_ref, gain_ref, bias_ref,
                ox_ref, om_ref, state_ref, meta_state_ref, *scratch):
    weights = (w_in_ref, w_gu_ref, b_gate_ref, norm_w_ref, w_out_ref, gain_ref, bias_ref)

    @pl.when((pl.program_id(0) == 0) & (pl.program_id(1) == 0))
    def _():
        meta_state_ref[...] = jnp.zeros_like(meta_state_ref)
        om_ref[...] = _gla_rows(m_ref[...], N_META, meta_state_ref, weights, scratch)

    @pl.when(pl.program_id(1) == 0)
    def _():
        state_ref[...] = meta_state_ref[...]

    ox_ref[0] = _gla_rows(x_ref[0], GLA_CHUNK, state_ref, weights, scratch)


def _gla_block(hx, hm, w_in, w_gu, b_gate, norm_w, w_out, gain, bias):
    bsz, seq, d = hx.shape
    tm = GLA_TILE
    assert seq % tm == 0 and tm % GLA_CHUNK == 0 and N_META <= GLA_CHUNK
    n_chunks = tm // GLA_CHUNK
    x_spec = pl.BlockSpec((1, tm, d), lambda b, t: (b, t, 0))
    state = pltpu.VMEM((GLA_HEADS, GLA_HEAD_K, GLA_HEAD_V), F32)
    return pl.pallas_call(
        _gla_kernel,
        grid=(bsz, seq // tm),
        in_specs=[x_spec, _resident((N_META, d)), _resident(w_in.shape), _resident((GLA_GATE_RANK, GLA_DK)),
                  _resident((1, GLA_DK)), _resident((1, GLA_HEAD_V)), _resident((GLA_DV, d)),
                  _resident((1, d)), _resident((1, d))],
        out_specs=(x_spec, pl.BlockSpec((N_META, d), lambda b, t: (0, 0))),
        out_shape=(jax.ShapeDtypeStruct((bsz, seq, d), F32), jax.ShapeDtypeStruct((N_META, d), F32)),
        scratch_shapes=[
            state, state,
            pltpu.VMEM((tm, GLA_DK), F32),
            pltpu.VMEM((tm, GLA_DK), F32),
            pltpu.VMEM((tm, GLA_DV), BF16),
            pltpu.VMEM((tm, GLA_DK), F32),
            pltpu.VMEM((tm, GLA_DK), F32),
            pltpu.VMEM((tm, GLA_DV), BF16),
            pltpu.VMEM((tm, GLA_DK), BF16),
            pltpu.VMEM((tm, GLA_DK), BF16),
            pltpu.VMEM((tm, GLA_DK), BF16),
            pltpu.VMEM((tm, GLA_DK), BF16),
            pltpu.VMEM((n_chunks, GLA_DK, GLA_CHUNK), BF16),
            pltpu.VMEM((GLA_HEADS, tm, GLA_CHUNK), BF16),
            pltpu.VMEM((n_chunks, GLA_HEADS, GLA_HEAD_K, GLA_HEAD_V), F32),
            pltpu.VMEM((tm, GLA_DV), F32),
        ],
        compiler_params=pltpu.CompilerParams(
            dimension_semantics=("arbitrary", "arbitrary"), vmem_limit_bytes=VMEM_LIMIT_BYTES),
        name="gla_block",
    )(hx, hm, w_in, w_gu, b_gate, norm_w, w_out, gain, bias)


def kernel(x, meta_tokens, ln_gain, ln_bias, ffn_w_in, ffn_w_out, conv_w_in, conv_w, conv_w_out,
           gla_w_in, gla_w_gate_up, gla_b_gate, gla_norm_w, gla_w_out):
    bsz, seq, d = x.shape
    hx = x
    hm = meta_tokens.astype(x.dtype)

    def ffn(hx, hm, i, half):
        ox, om = _ffn_block(hx.reshape(bsz * seq, d), hm, ffn_w_in, ffn_w_out,
                            ln_gain[i, 2 * half][None], ln_bias[i, 2 * half][None], (i, half))
        return ox.reshape(bsz, seq, d), om

    for i in range(DEPTH):
        hx, hm = ffn(hx, hm, i, 0)
        j = i // 2
        gain, bias = ln_gain[i, 1][None], ln_bias[i, 1][None]
        if i % 2 == 0:
            hx, hm = _conv_block(hx, hm, conv_w_in[j].astype(BF16), conv_w[j], conv_w_out[j].astype(BF16),
                                 gain, bias)
        else:
            hx, hm = _gla_block(hx, hm, gla_w_in[j].astype(BF16), gla_w_gate_up[j].astype(BF16),
                                gla_b_gate[j][None], gla_norm_w[j][None], gla_w_out[j].astype(BF16), gain, bias)
        hx, hm = ffn(hx, hm if i + 1 < DEPTH else None, i, 1)
    return hx
```

```python
import jax
import jax.numpy as jnp
from jax import lax
from jax.experimental import pallas as pl
from jax.experimental.pallas import tpu as pltpu

D_MODEL = 1024
DEPTH = 2
N_META = 16
CONV_WIDTH = 3
GLA_HEADS = 4
GLA_DK = D_MODEL // 2
GLA_DV = D_MODEL
GLA_HEAD_K = GLA_DK // GLA_HEADS
GLA_HEAD_V = GLA_DV // GLA_HEADS
GLA_GATE_RANK = 16
GLA_GATE_NORMALIZER = 16.0
GLA_CHUNK = 64
DEEPNORM_ALPHA = (2.0 * DEPTH) ** 0.25
LN_EPS = 1e-5
RMS_EPS = 1e-6

CONV_CARRY_ROWS = 8
FFN_TILE = 512
FFN_STAGE_COLS = 1408
FFN_STAGE_SLOTS = 3
CONV_SUB_TILE = 256
CONV_SUBTILES = 2
GLA_TILE = 512
GLA_OUT_MIN_ROWS = 256
VMEM_LIMIT_BYTES = 56 * 1024 * 1024

F32 = jnp.float32
BF16 = jnp.bfloat16


def _dot(a, b):
    return jnp.dot(a, b, preferred_element_type=F32)


def _layer_norm(y, gain, bias):
    mu = jnp.mean(y, axis=-1, keepdims=True)
    yc = y - mu
    var = jnp.mean(yc * yc, axis=-1, keepdims=True)
    return yc * lax.rsqrt(var + LN_EPS) * gain + bias


def _resident(shape):
    zeros = (0,) * len(shape)
    return pl.BlockSpec(shape, lambda *_: zeros, pipeline_mode=pl.Buffered(1))


def _ffn_residual(x, w_in_ref, w_out_ref, before_gate=None, before_up=None, before_out=None):
    f = w_out_ref.shape[0]
    xb = x.astype(BF16)
    if before_gate is not None:
        before_gate()
    gate = _dot(xb, w_in_ref[:, :f])
    if before_up is not None:
        before_up()
    up = _dot(xb, w_in_ref[:, f:])
    act = (gate * jax.nn.sigmoid(gate) * up).astype(BF16)
    if before_out is not None:
        before_out()
    return DEEPNORM_ALPHA * x + 0.5 * _dot(act, w_out_ref[...])


def _ffn_kernel(with_meta, n_tiles, sel, *refs):
    if with_meta:
        (x_ref, m_ref, w_in_hbm, w_out_hbm, gain_ref, bias_ref, ox_ref, om_ref,
         y_ref, w_in_ref, w_out_ref, stage_ref, sem) = refs
    else:
        (x_ref, w_in_hbm, w_out_hbm, gain_ref, bias_ref, ox_ref,
         y_ref, w_in_ref, w_out_ref, stage_ref, sem) = refs
    i = pl.program_id(0)
    d, cols = stage_ref.shape[1], stage_ref.shape[2]
    f = w_out_ref.shape[0]
    layer, half = sel
    pieces = []
    for c0 in range(0, 2 * f, cols):
        pieces.append((w_in_hbm.at[layer, half, :, pl.ds(c0, cols)], (slice(0, d), slice(0, cols)),
                       w_in_ref.at[:, pl.ds(c0, cols)]))
    for r0 in range(0, f, d):
        rows = min(d, f - r0)
        pieces.append((w_out_hbm.at[layer, half, pl.ds(r0, rows), :], (slice(0, rows), slice(0, d)),
                       w_out_ref.at[pl.ds(r0, rows), :]))
    n_in = 2 * f // cols

    slots = stage_ref.shape[0]

    def copy(k):
        src, view, _ = pieces[k]
        return pltpu.make_async_copy(src, stage_ref.at[k % slots, view[0], view[1]], sem.at[k])

    def land(k):
        _, view, dst = pieces[k]
        copy(k).wait()
        dst[...] = stage_ref[k % slots, view[0], view[1]].astype(BF16)
        if k + slots < len(pieces):
            copy(k + slots).start()

    def land_range(lo, hi):
        def run():
            for k in range(lo, hi):
                land(k)
        return run

    def normalise_previous():
        ox_ref[...] = _layer_norm(y_ref[...], gain_ref[...], bias_ref[...])

    def form_current(*hooks):
        y_ref[...] = _ffn_residual(x_ref[...], w_in_ref, w_out_ref, *hooks)

    @pl.when(i == 0)
    def _():
        for k in range(slots):
            copy(k).start()
        form_current(land_range(0, n_in // 2), land_range(n_in // 2, n_in), land_range(n_in, len(pieces)))
        if with_meta:
            om_ref[...] = _layer_norm(_ffn_residual(m_ref[...], w_in_ref, w_out_ref),
                                      gain_ref[...], bias_ref[...])

    @pl.when((i > 0) & (i < n_tiles))
    def _():
        normalise_previous()
        form_current()

    @pl.when(i == n_tiles)
    def _():
        normalise_previous()


def _ffn_block(hx, hm, w_in, w_out, gain, bias, sel):
    n, d = hx.shape
    f = w_out.shape[2]
    tm = FFN_TILE
    assert n % tm == 0 and (2 * f) % FFN_STAGE_COLS == 0 and f % FFN_STAGE_COLS == 0 and d <= FFN_STAGE_COLS
    n_tiles = n // tm
    n_pieces = 2 * f // FFN_STAGE_COLS + pl.cdiv(f, d)
    with_meta = hm is not None
    x_spec = pl.BlockSpec((tm, d), lambda i: (jnp.minimum(i, n_tiles - 1), 0))
    o_spec = pl.BlockSpec((tm, d), lambda i: (jnp.maximum(i - 1, 0), 0))
    hbm = pl.BlockSpec(memory_space=pl.ANY)
    w_specs = [hbm, hbm, _resident((1, d)), _resident((1, d))]
    params = pltpu.CompilerParams(dimension_semantics=("arbitrary",), vmem_limit_bytes=VMEM_LIMIT_BYTES)
    scratch = [pltpu.VMEM((tm, d), F32),
               pltpu.VMEM((d, 2 * f), BF16), pltpu.VMEM((f, d), BF16),
               pltpu.VMEM((FFN_STAGE_SLOTS, d, FFN_STAGE_COLS), F32),
               pltpu.SemaphoreType.DMA((n_pieces,))]
    if not with_meta:
        return pl.pallas_call(
            lambda *refs: _ffn_kernel(False, n_tiles, sel, *refs),
            grid=(n_tiles + 1,), in_specs=[x_spec] + w_specs, out_specs=o_spec,
            out_shape=jax.ShapeDtypeStruct((n, d), F32), scratch_shapes=scratch,
            compiler_params=params, name="ffn_block",
        )(hx, w_in, w_out, gain, bias), None
    return pl.pallas_call(
        lambda *refs: _ffn_kernel(True, n_tiles, sel, *refs),
        grid=(n_tiles + 1,), in_specs=[x_spec, _resident((N_META, d))] + w_specs,
        out_specs=(o_spec, pl.BlockSpec((N_META, d), lambda i: (0, 0))),
        out_shape=(jax.ShapeDtypeStruct((n, d), F32), jax.ShapeDtypeStruct((N_META, d), F32)),
        scratch_shapes=scratch, compiler_params=params, name="ffn_meta_block",
    )(hx, hm, w_in, w_out, gain, bias)


def _conv_rows(x, prev, w_in_ref, w_conv_ref, w_out_ref, gain_ref, bias_ref):
    n, d = x.shape
    xb = x.astype(BF16)
    u = _dot(xb, w_in_ref[:, d:2 * d]) * _dot(xb, w_in_ref[:, 2 * d:])
    bgate = _dot(xb, w_in_ref[:, :d])
    row = lax.broadcasted_iota(jnp.int32, (n, 1), 0)
    last = prev[CONV_CARRY_ROWS - 1:CONV_CARRY_ROWS]
    u1 = jnp.where(row == 0, last, pltpu.roll(u, 1, 0))
    u2 = jnp.where(row == 0, prev[CONV_CARRY_ROWS - 2:CONV_CARRY_ROWS - 1],
                   jnp.where(row == 1, last, pltpu.roll(u, 2, 0)))
    w = w_conv_ref[...]
    conv = w[0:1] * u2 + w[1:2] * u1 + w[2:3] * u
    mix = _dot((bgate * conv).astype(BF16), w_out_ref[...])
    out = _layer_norm(DEEPNORM_ALPHA * x + mix, gain_ref[...], bias_ref[...])
    return out, u[n - CONV_CARRY_ROWS:]


def _conv_kernel(x_ref, m_ref, w_in_ref, w_conv_ref, w_out_ref, gain_ref, bias_ref,
                 ox_ref, om_ref, carry_ref, meta_carry_ref):
    weights = (w_in_ref, w_conv_ref, w_out_ref, gain_ref, bias_ref)

    @pl.when((pl.program_id(0) == 0) & (pl.program_id(1) == 0))
    def _():
        zeros = jnp.zeros((CONV_CARRY_ROWS, x_ref.shape[2]), F32)
        om_ref[...], meta_carry_ref[...] = _conv_rows(m_ref[...], zeros, *weights)

    @pl.when(pl.program_id(1) == 0)
    def _():
        carry_ref[...] = meta_carry_ref[...]

    prev = carry_ref[...]
    for s in range(CONV_SUBTILES):
        rows = slice(s * CONV_SUB_TILE, (s + 1) * CONV_SUB_TILE)
        ox_ref[0, rows, :], prev = _conv_rows(x_ref[0, rows, :], prev, *weights)
    carry_ref[...] = prev


def _conv_block(hx, hm, w_in, w_conv, w_out, gain, bias):
    bsz, seq, d = hx.shape
    tm = CONV_SUB_TILE * CONV_SUBTILES
    assert seq % tm == 0 and CONV_WIDTH - 1 <= CONV_CARRY_ROWS <= N_META
    x_spec = pl.BlockSpec((1, tm, d), lambda b, t: (b, t, 0))
    return pl.pallas_call(
        _conv_kernel,
        grid=(bsz, seq // tm),
        in_specs=[x_spec, _resident((N_META, d)), _resident((d, 3 * d)), _resident((CONV_WIDTH, d)),
                  _resident((d, d)), _resident((1, d)), _resident((1, d))],
        out_specs=(x_spec, pl.BlockSpec((N_META, d), lambda b, t: (0, 0))),
        out_shape=(jax.ShapeDtypeStruct((bsz, seq, d), F32), jax.ShapeDtypeStruct((N_META, d), F32)),
        scratch_shapes=[pltpu.VMEM((CONV_CARRY_ROWS, d), F32), pltpu.VMEM((CONV_CARRY_ROWS, d), F32)],
        compiler_params=pltpu.CompilerParams(
            dimension_semantics=("arbitrary", "arbitrary"), vmem_limit_bytes=VMEM_LIMIT_BYTES),
        name="conv_block",
    )(hx, hm, w_in, w_conv, w_out, gain, bias)


def _split3_bf16(a):
    hi = a.astype(BF16)
    r = a - hi.astype(F32)
    mid = r.astype(BF16)
    lo = (r - mid.astype(F32)).astype(BF16)
    return hi, mid, lo


def _gla_rows(x, c, state_ref, weights, scratch):
    (w_in_ref, w_gu_ref, b_gate_ref, norm_w_ref, w_out_ref, gain_ref, bias_ref) = weights
    (q_ref, k_ref, v_ref, la_ref, b_ref, og_ref, qin_ref, qt_ref, kt_ref, kst_ref, kstT_ref,
     p_ref, upd_ref, gate_ref) = scratch
    n = x.shape[0]
    n_chunks = n // c
    all_rows = slice(0, n)
    o_k, o_v, o_g, o_gz = GLA_DK, 2 * GLA_DK, 2 * GLA_DK + GLA_DV, 2 * GLA_DK + 2 * GLA_DV
    xb = x.astype(BF16)
    gz = _dot(xb, w_in_ref[:, o_gz:])
    q_ref[all_rows, :] = _dot(xb, w_in_ref[:, :o_k]) * (GLA_HEAD_K ** -0.5)
    z = _dot(gz.astype(BF16), w_gu_ref[...]) + b_gate_ref[...]
    k_ref[all_rows, :] = _dot(xb, w_in_ref[:, o_k:o_v])
    v_ref[all_rows, :] = _dot(xb, w_in_ref[:, o_v:o_g]).astype(BF16)
    la_ref[all_rows, :] = (jnp.minimum(z, 0.0) - jnp.log1p(jnp.exp(-jnp.abs(z)))) * (1.0 / GLA_GATE_NORMALIZER)

    ri = lax.broadcasted_iota(jnp.int32, (c, c), 0)
    ci = lax.broadcasted_iota(jnp.int32, (c, c), 1)
    causal = ci <= ri
    tril = causal.astype(BF16)
    head_k = [slice(hd * GLA_HEAD_K, (hd + 1) * GLA_HEAD_K) for hd in range(GLA_HEADS)]
    head_v = [slice(hd * GLA_HEAD_V, (hd + 1) * GLA_HEAD_V) for hd in range(GLA_HEADS)]
    chunk_rows = [slice(ic * c, (ic + 1) * c) for ic in range(n_chunks)]

    for rows in chunk_rows:
        hi, mid, lo = _split3_bf16(la_ref[rows, :])
        b_ref[rows, :] = _dot(tril, hi) + _dot(tril, mid) + _dot(tril, lo)
    gate = _dot(xb, w_in_ref[:, o_g:o_gz])
    gate_ref[all_rows, :] = gate * jax.nn.sigmoid(gate)
    decays = []
    for rows in chunk_rows:
        b = b_ref[rows, :]
        b_mid = b[c // 2:c // 2 + 1]
        b_last = b[c - 1:c]
        q_t = q_ref[rows, :] * jnp.exp(b - b_mid)
        k_t = k_ref[rows, :] * jnp.exp(b_mid - b)
        qin_ref[rows, :] = (q_t * jnp.exp(b_mid)).astype(BF16)
        kst_ref[rows, :] = (k_t * jnp.exp(b_last - b_mid)).astype(BF16)
        qt_ref[rows, :] = q_t.astype(BF16)
        kt_ref[rows, :] = k_t.astype(BF16)
        decays.append(jnp.exp(b_last))
    for ic, rows in enumerate(chunk_rows):
        kstT_ref[ic, :, 0:c] = jnp.transpose(kst_ref[rows, :])
    for ic, rows in enumerate(chunk_rows):
        for hd in range(GLA_HEADS):
            scores = lax.dot_general(qt_ref[rows, head_k[hd]], kt_ref[rows, head_k[hd]],
                                     (((1,), (1,)), ((), ())), preferred_element_type=F32)
            p_ref[hd, rows, 0:c] = jnp.where(causal, scores, 0.0).astype(BF16)
            upd_ref[ic, hd] = _dot(kstT_ref[ic, head_k[hd], 0:c], v_ref[rows, head_v[hd]])

    pad = [jnp.zeros_like(decays[0])] * (-n_chunks % 8)
    dec_rows = jnp.concatenate(decays + pad, axis=0)
    dec_cols = [jnp.transpose(dec_rows[:, head_k[hd]]) for hd in range(GLA_HEADS)]

    norm_w = norm_w_ref[...]
    for ic, rows in enumerate(chunk_rows):
        for hd in range(GLA_HEADS):
            s_prev = state_ref[hd]
            o = (_dot(p_ref[hd, rows, 0:c], v_ref[rows, head_v[hd]])
                 + _dot(qin_ref[rows, head_k[hd]], s_prev.astype(BF16)))
            state_ref[hd] = dec_cols[hd][:, ic:ic + 1] * s_prev + upd_ref[ic, hd]
            o = o * lax.rsqrt(jnp.mean(o * o, axis=-1, keepdims=True) + RMS_EPS) * norm_w
            og_ref[rows, head_v[hd]] = (o * gate_ref[rows, head_v[hd]]).astype(BF16)

    halves = (slice(0, n // 2), slice(n // 2, n)) if n >= 2 * GLA_OUT_MIN_ROWS else (all_rows,)
    outs = []
    for r in halves:
        mix = _dot(og_ref[r, :], w_out_ref[...])
        outs.append(_layer_norm(DEEPNORM_ALPHA * x[r] + mix, gain_ref[...], bias_ref[...]))
    return jnp.concatenate(outs, axis=0)


def _gla_kernel(x_ref, m_ref, w_in_ref, w_gu_ref, b_gate_ref, norm_w_ref, w_out_ref, gain_ref, bias_ref,
                ox_ref, om_ref, state_ref, meta_state_ref, *scratch):
    weights = (w_in_ref, w_gu_ref, b_gate_ref, norm_w_ref, w_out_ref, gain_ref, bias_ref)

    @pl.when((pl.program_id(0) == 0) & (pl.program_id(1) == 0))
    def _():
        meta_state_ref[...] = jnp.zeros_like(meta_state_ref)
        om_ref[...] = _gla_rows(m_ref[...], N_META, meta_state_ref, weights, scratch)

    @pl.when(pl.program_id(1) == 0)
    def _():
        state_ref[...] = meta_state_ref[...]

    ox_ref[0] = _gla_rows(x_ref[0], GLA_CHUNK, state_ref, weights, scratch)


def _gla_block(hx, hm, w_in, w_gu, b_gate, norm_w, w_out, gain, bias):
    bsz, seq, d = hx.shape
    tm = GLA_TILE
    assert seq % tm == 0 and tm % GLA_CHUNK == 0 and N_META <= GLA_CHUNK
    n_chunks = tm // GLA_CHUNK
    x_spec = pl.BlockSpec((1, tm, d), lambda b, t: (b, t, 0))
    state = pltpu.VMEM((GLA_HEADS, GLA_HEAD_K, GLA_HEAD_V), F32)
    return pl.pallas_call(
        _gla_kernel,
        grid=(bsz, seq // tm),
        in_specs=[x_spec, _resident((N_META, d)), _resident(w_in.shape), _resident((GLA_GATE_RANK, GLA_DK)),
                  _resident((1, GLA_DK)), _resident((1, GLA_HEAD_V)), _resident((GLA_DV, d)),
                  _resident((1, d)), _resident((1, d))],
        out_specs=(x_spec, pl.BlockSpec((N_META, d), lambda b, t: (0, 0))),
        out_shape=(jax.ShapeDtypeStruct((bsz, seq, d), F32), jax.ShapeDtypeStruct((N_META, d), F32)),
        scratch_shapes=[
            state, state,
            pltpu.VMEM((tm, GLA_DK), F32),
            pltpu.VMEM((tm, GLA_DK), F32),
            pltpu.VMEM((tm, GLA_DV), BF16),
            pltpu.VMEM((tm, GLA_DK), F32),
            pltpu.VMEM((tm, GLA_DK), F32),
            pltpu.VMEM((tm, GLA_DV), BF16),
            pltpu.VMEM((tm, GLA_DK), BF16),
            pltpu.VMEM((tm, GLA_DK), BF16),
            pltpu.VMEM((tm, GLA_DK), BF16),
            pltpu.VMEM((tm, GLA_DK), BF16),
            pltpu.VMEM((n_chunks, GLA_DK, GLA_CHUNK), BF16),
            pltpu.VMEM((GLA_HEADS, tm, GLA_CHUNK), BF16),
            pltpu.VMEM((n_chunks, GLA_HEADS, GLA_HEAD_K, GLA_HEAD_V), F32),
            pltpu.VMEM((tm, GLA_DV), F32),
        ],
        compiler_params=pltpu.CompilerParams(
            dimension_semantics=("arbitrary", "arbitrary"), vmem_limit_bytes=VMEM_LIMIT_BYTES),
        name="gla_block",
    )(hx, hm, w_in, w_gu, b_gate, norm_w, w_out, gain, bias)


def kernel(x, meta_tokens, ln_gain, ln_bias, ffn_w_in, ffn_w_out, conv_w_in, conv_w, conv_w_out,
           gla_w_in, gla_w_gate_up, gla_b_gate, gla_norm_w, gla_w_out):
    bsz, seq, d = x.shape
    hx = x
    hm = meta_tokens.astype(x.dtype)

    def ffn(hx, hm, i, half):
        ox, om = _ffn_block(hx.reshape(bsz * seq, d), hm, ffn_w_in, ffn_w_out,
                            ln_gain[i, 2 * half][None], ln_bias[i, 2 * half][None], (i, half))
        return ox.reshape(bsz, seq, d), om

    for i in range(DEPTH):
        hx, hm = ffn(hx, hm, i, 0)
        j = i // 2
        gain, bias = ln_gain[i, 1][None], ln_bias[i, 1][None]
        if i % 2 == 0:
            hx, hm = _conv_block(hx, hm, conv_w_in[j].astype(BF16), conv_w[j], conv_w_out[j].astype(BF16),
                                 gain, bias)
        else:
            hx, hm = _gla_block(hx, hm, gla_w_in[j].astype(BF16), gla_w_gate_up[j].astype(BF16),
                                gla_b_gate[j][None], gla_norm_w[j][None], gla_w_out[j].astype(BF16), gain, bias)
        hx, hm = ffn(hx, hm if i + 1 < DEPTH else None, i, 1)
    return hx
```

```python
import jax
import jax.numpy as jnp
from jax import lax
from jax.experimental import pallas as pl
from jax.experimental.pallas import tpu as pltpu

D_MODEL = 1024
DEPTH = 2
N_META = 16
CONV_WIDTH = 3
GLA_HEADS = 4
GLA_DK = D_MODEL // 2
GLA_DV = D_MODEL
GLA_HEAD_K = GLA_DK // GLA_HEADS
GLA_HEAD_V = GLA_DV // GLA_HEADS
GLA_GATE_RANK = 16
GLA_GATE_NORMALIZER = 16.0
GLA_CHUNK = 64
DEEPNORM_ALPHA = (2.0 * DEPTH) ** 0.25
LN_EPS = 1e-5
RMS_EPS = 1e-6

CONV_CARRY_ROWS = 8
FFN_TILE = 512
FFN_STAGE_COLS = 1408
FFN_STAGE_SLOTS = 3
CONV_SUB_TILE = 256
CONV_SUBTILES = 4
GLA_TILE = 512
GLA_OUT_MIN_ROWS = 256
VMEM_LIMIT_BYTES = 56 * 1024 * 1024

F32 = jnp.float32
BF16 = jnp.bfloat16


def _dot(a, b):
    return jnp.dot(a, b, preferred_element_type=F32)


def _layer_norm(y, gain, bias):
    mu = jnp.mean(y, axis=-1, keepdims=True)
    yc = y - mu
    var = jnp.mean(yc * yc, axis=-1, keepdims=True)
    return yc * lax.rsqrt(var + LN_EPS) * gain + bias


def _resident(shape):
    zeros = (0,) * len(shape)
    return pl.BlockSpec(shape, lambda *_: zeros, pipeline_mode=pl.Buffered(1))


def _ffn_residual(x, w_in_ref, w_out_ref, before_gate=None, before_up=None, before_out=None):
    f = w_out_ref.shape[0]
    xb = x.astype(BF16)
    if before_gate is not None:
        before_gate()
    gate = _dot(xb, w_in_ref[:, :f])
    if before_up is not None:
        before_up()
    up = _dot(xb, w_in_ref[:, f:])
    act = (gate * jax.nn.sigmoid(gate) * up).astype(BF16)
    if before_out is not None:
        before_out()
    return DEEPNORM_ALPHA * x + 0.5 * _dot(act, w_out_ref[...])


def _ffn_kernel(with_meta, n_tiles, sel, *refs):
    if with_meta:
        (x_ref, m_ref, w_in_hbm, w_out_hbm, gain_ref, bias_ref, ox_ref, om_ref,
         y_ref, w_in_ref, w_out_ref, stage_ref, sem) = refs
    else:
        (x_ref, w_in_hbm, w_out_hbm, gain_ref, bias_ref, ox_ref,
         y_ref, w_in_ref, w_out_ref, stage_ref, sem) = refs
    i = pl.program_id(0)
    d, cols = stage_ref.shape[1], stage_ref.shape[2]
    f = w_out_ref.shape[0]
    layer, half = sel
    pieces = []
    for c0 in range(0, 2 * f, cols):
        pieces.append((w_in_hbm.at[layer, half, :, pl.ds(c0, cols)], (slice(0, d), slice(0, cols)),
                       w_in_ref.at[:, pl.ds(c0, cols)]))
    for r0 in range(0, f, d):
        rows = min(d, f - r0)
        pieces.append((w_out_hbm.at[layer, half, pl.ds(r0, rows), :], (slice(0, rows), slice(0, d)),
                       w_out_ref.at[pl.ds(r0, rows), :]))
    n_in = 2 * f // cols

    slots = stage_ref.shape[0]

    def copy(k):
        src, view, _ = pieces[k]
        return pltpu.make_async_copy(src, stage_ref.at[k % slots, view[0], view[1]], sem.at[k])

    def land(k):
        _, view, dst = pieces[k]
        copy(k).wait()
        dst[...] = stage_ref[k % slots, view[0], view[1]].astype(BF16)
        if k + slots < len(pieces):
            copy(k + slots).start(priority=(k + slots) % 2)

    def land_range(lo, hi):
        def run():
            for k in range(lo, hi):
                land(k)
        return run

    def normalise_previous():
        ox_ref[...] = _layer_norm(y_ref[...], gain_ref[...], bias_ref[...])

    def form_current(*hooks):
        y_ref[...] = _ffn_residual(x_ref[...], w_in_ref, w_out_ref, *hooks)

    @pl.when(i == 0)
    def _():
        for k in range(slots):
            copy(k).start(priority=k % 2)
        form_current(land_range(0, n_in // 2), land_range(n_in // 2, n_in), land_range(n_in, len(pieces)))
        if with_meta:
            om_ref[...] = _layer_norm(_ffn_residual(m_ref[...], w_in_ref, w_out_ref),
                                      gain_ref[...], bias_ref[...])

    @pl.when((i > 0) & (i < n_tiles))
    def _():
        normalise_previous()
        form_current()

    @pl.when(i == n_tiles)
    def _():
        normalise_previous()


def _ffn_block(hx, hm, w_in, w_out, gain, bias, sel):
    n, d = hx.shape
    f = w_out.shape[2]
    tm = FFN_TILE
    assert n % tm == 0 and (2 * f) % FFN_STAGE_COLS == 0 and f % FFN_STAGE_COLS == 0 and d <= FFN_STAGE_COLS
    n_tiles = n // tm
    n_pieces = 2 * f // FFN_STAGE_COLS + pl.cdiv(f, d)
    with_meta = hm is not None
    x_spec = pl.BlockSpec((tm, d), lambda i: (jnp.minimum(i, n_tiles - 1), 0))
    o_spec = pl.BlockSpec((tm, d), lambda i: (jnp.maximum(i - 1, 0), 0))
    hbm = pl.BlockSpec(memory_space=pl.ANY)
    w_specs = [hbm, hbm, _resident((1, d)), _resident((1, d))]
    params = pltpu.CompilerParams(dimension_semantics=("arbitrary",), vmem_limit_bytes=VMEM_LIMIT_BYTES)
    scratch = [pltpu.VMEM((tm, d), F32),
               pltpu.VMEM((d, 2 * f), BF16), pltpu.VMEM((f, d), BF16),
               pltpu.VMEM((FFN_STAGE_SLOTS, d, FFN_STAGE_COLS), F32),
               pltpu.SemaphoreType.DMA((n_pieces,))]
    if not with_meta:
        return pl.pallas_call(
            lambda *refs: _ffn_kernel(False, n_tiles, sel, *refs),
            grid=(n_tiles + 1,), in_specs=[x_spec] + w_specs, out_specs=o_spec,
            out_shape=jax.ShapeDtypeStruct((n, d), F32), scratch_shapes=scratch,
            compiler_params=params, name="ffn_block",
        )(hx, w_in, w_out, gain, bias), None
    return pl.pallas_call(
        lambda *refs: _ffn_kernel(True, n_tiles, sel, *refs),
        grid=(n_tiles + 1,), in_specs=[x_spec, _resident((N_META, d))] + w_specs,
        out_specs=(o_spec, pl.BlockSpec((N_META, d), lambda i: (0, 0))),
        out_shape=(jax.ShapeDtypeStruct((n, d), F32), jax.ShapeDtypeStruct((N_META, d), F32)),
        scratch_shapes=scratch, compiler_params=params, name="ffn_meta_block",
    )(hx, hm, w_in, w_out, gain, bias)


def _conv_rows(x, prev, w_in_ref, w_conv_ref, w_out_ref, gain_ref, bias_ref):
    n, d = x.shape
    xb = x.astype(BF16)
    u = _dot(xb, w_in_ref[:, d:2 * d]) * _dot(xb, w_in_ref[:, 2 * d:])
    bgate = _dot(xb, w_in_ref[:, :d])
    row = lax.broadcasted_iota(jnp.int32, (n, 1), 0)
    last = prev[CONV_CARRY_ROWS - 1:CONV_CARRY_ROWS]
    u1 = jnp.where(row == 0, last, pltpu.roll(u, 1, 0))
    u2 = jnp.where(row == 0, prev[CONV_CARRY_ROWS - 2:CONV_CARRY_ROWS - 1],
                   jnp.where(row == 1, last, pltpu.roll(u, 2, 0)))
    w = w_conv_ref[...]
    conv = w[0:1] * u2 + w[1:2] * u1 + w[2:3] * u
    mix = _dot((bgate * conv).astype(BF16), w_out_ref[...])
    out = _layer_norm(DEEPNORM_ALPHA * x + mix, gain_ref[...], bias_ref[...])
    return out, u[n - CONV_CARRY_ROWS:]


def _conv_kernel(x_ref, m_ref, w_in_ref, w_conv_ref, w_out_ref, gain_ref, bias_ref,
                 ox_ref, om_ref, carry_ref, meta_carry_ref):
    weights = (w_in_ref, w_conv_ref, w_out_ref, gain_ref, bias_ref)

    @pl.when((pl.program_id(0) == 0) & (pl.program_id(1) == 0))
    def _():
        zeros = jnp.zeros((CONV_CARRY_ROWS, x_ref.shape[2]), F32)
        om_ref[...], meta_carry_ref[...] = _conv_rows(m_ref[...], zeros, *weights)

    @pl.when(pl.program_id(1) == 0)
    def _():
        carry_ref[...] = meta_carry_ref[...]

    prev = carry_ref[...]
    for s in range(CONV_SUBTILES):
        rows = slice(s * CONV_SUB_TILE, (s + 1) * CONV_SUB_TILE)
        ox_ref[0, rows, :], prev = _conv_rows(x_ref[0, rows, :], prev, *weights)
    carry_ref[...] = prev


def _conv_block(hx, hm, w_in, w_conv, w_out, gain, bias):
    bsz, seq, d = hx.shape
    tm = CONV_SUB_TILE * CONV_SUBTILES
    assert seq % tm == 0 and CONV_WIDTH - 1 <= CONV_CARRY_ROWS <= N_META
    x_spec = pl.BlockSpec((1, tm, d), lambda b, t: (b, t, 0))
    return pl.pallas_call(
        _conv_kernel,
        grid=(bsz, seq // tm),
        in_specs=[x_spec, _resident((N_META, d)), _resident((d, 3 * d)), _resident((CONV_WIDTH, d)),
                  _resident((d, d)), _resident((1, d)), _resident((1, d))],
        out_specs=(x_spec, pl.BlockSpec((N_META, d), lambda b, t: (0, 0))),
        out_shape=(jax.ShapeDtypeStruct((bsz, seq, d), F32), jax.ShapeDtypeStruct((N_META, d), F32)),
        scratch_shapes=[pltpu.VMEM((CONV_CARRY_ROWS, d), F32), pltpu.VMEM((CONV_CARRY_ROWS, d), F32)],
        compiler_params=pltpu.CompilerParams(
            dimension_semantics=("arbitrary", "arbitrary"), vmem_limit_bytes=VMEM_LIMIT_BYTES),
        name="conv_block",
    )(hx, hm, w_in, w_conv, w_out, gain, bias)


def _split3_bf16(a):
    hi = a.astype(BF16)
    r = a - hi.astype(F32)
    mid = r.astype(BF16)
    lo = (r - mid.astype(F32)).astype(BF16)
    return hi, mid, lo


def _gla_rows(x, c, state_ref, weights, scratch):
    (w_in_ref, w_gu_ref, b_gate_ref, norm_w_ref, w_out_ref, gain_ref, bias_ref) = weights
    (q_ref, k_ref, v_ref, la_ref, b_ref, og_ref, qin_ref, qt_ref, kt_ref, kst_ref, kstT_ref,
     p_ref, upd_ref, gate_ref) = scratch
    n = x.shape[0]
    n_chunks = n // c
    all_rows = slice(0, n)
    o_k, o_v, o_g, o_gz = GLA_DK, 2 * GLA_DK, 2 * GLA_DK + GLA_DV, 2 * GLA_DK + 2 * GLA_DV
    xb = x.astype(BF16)
    gz = _dot(xb, w_in_ref[:, o_gz:])
    q_ref[all_rows, :] = _dot(xb, w_in_ref[:, :o_k]) * (GLA_HEAD_K ** -0.5)
    z = _dot(gz.astype(BF16), w_gu_ref[...]) + b_gate_ref[...]
    k_ref[all_rows, :] = _dot(xb, w_in_ref[:, o_k:o_v])
    v_ref[all_rows, :] = _dot(xb, w_in_ref[:, o_v:o_g]).astype(BF16)
    la_ref[all_rows, :] = (jnp.minimum(z, 0.0) - jnp.log1p(jnp.exp(-jnp.abs(z)))) * (1.0 / GLA_GATE_NORMALIZER)

    ri = lax.broadcasted_iota(jnp.int32, (c, c), 0)
    ci = lax.broadcasted_iota(jnp.int32, (c, c), 1)
    causal = ci <= ri
    tril = causal.astype(BF16)
    head_k = [slice(hd * GLA_HEAD_K, (hd + 1) * GLA_HEAD_K) for hd in range(GLA_HEADS)]
    head_v = [slice(hd * GLA_HEAD_V, (hd + 1) * GLA_HEAD_V) for hd in range(GLA_HEADS)]
    chunk_rows = [slice(ic * c, (ic + 1) * c) for ic in range(n_chunks)]

    for rows in chunk_rows:
        hi, mid, lo = _split3_bf16(la_ref[rows, :])
        b_ref[rows, :] = _dot(tril, hi) + _dot(tril, mid) + _dot(tril, lo)
    gate = _dot(xb, w_in_ref[:, o_g:o_gz])
    gate_ref[all_rows, :] = gate * jax.nn.sigmoid(gate)
    decays = []
    for rows in chunk_rows:
        b = b_ref[rows, :]
        b_mid = b[c // 2:c // 2 + 1]
        b_last = b[c - 1:c]
        q_t = q_ref[rows, :] * jnp.exp(b - b_mid)
        k_t = k_ref[rows, :] * jnp.exp(b_mid - b)
        qin_ref[rows, :] = (q_t * jnp.exp(b_mid)).astype(BF16)
        kst_ref[rows, :] = (k_t * jnp.exp(b_last - b_mid)).astype(BF16)
        qt_ref[rows, :] = q_t.astype(BF16)
        kt_ref[rows, :] = k_t.astype(BF16)
        decays.append(jnp.exp(b_last))
    for ic, rows in enumerate(chunk_rows):
        kstT_ref[ic, :, 0:c] = jnp.transpose(kst_ref[rows, :])
    for ic, rows in enumerate(chunk_rows):
        for hd in range(GLA_HEADS):
            scores = lax.dot_general(qt_ref[rows, head_k[hd]], kt_ref[rows, head_k[hd]],
                                     (((1,), (1,)), ((), ())), preferred_element_type=F32)
            p_ref[hd, rows, 0:c] = jnp.where(causal, scores, 0.0).astype(BF16)
            upd_ref[ic, hd] = _dot(kstT_ref[ic, head_k[hd], 0:c], v_ref[rows, head_v[hd]])

    pad = [jnp.zeros_like(decays[0])] * (-n_chunks % 8)
    dec_rows = jnp.concatenate(decays + pad, axis=0)
    dec_cols = [jnp.transpose(dec_rows[:, head_k[hd]]) for hd in range(GLA_HEADS)]

    norm_w = norm_w_ref[...]
    for ic, rows in enumerate(chunk_rows):
        for hd in range(GLA_HEADS):
            s_prev = state_ref[hd]
            o = (_dot(p_ref[hd, rows, 0:c], v_ref[rows, head_v[hd]])
                 + _dot(qin_ref[rows, head_k[hd]], s_prev.astype(BF16)))
            state_ref[hd] = dec_cols[hd][:, ic:ic + 1] * s_prev + upd_ref[ic, hd]
            o = o * lax.rsqrt(jnp.mean(o * o, axis=-1, keepdims=True) + RMS_EPS) * norm_w
            og_ref[rows, head_v[hd]] = (o * gate_ref[rows, head_v[hd]]).astype(BF16)

    halves = (slice(0, n // 2), slice(n // 2, n)) if n >= 2 * GLA_OUT_MIN_ROWS else (all_rows,)
    outs = []
    for r in halves:
        mix = _dot(og_ref[r, :], w_out_ref[...])
        outs.append(_layer_norm(DEEPNORM_ALPHA * x[r] + mix, gain_ref[...], bias_ref[...]))
    return jnp.concatenate(outs, axis=0)


def _gla_kernel(x_ref, m_ref, w_in_ref, w_gu_ref, b_gate_ref, norm_w_ref, w_out_ref, gain_ref, bias_ref,
                ox_ref, om_ref, state_ref, meta_state_ref, *scratch):
    weights = (w_in_ref, w_gu_ref, b_gate_ref, norm_w_ref, w_out_ref, gain_ref, bias_ref)

    @pl.when((pl.program_id(0) == 0) & (pl.program_id(1) == 0))
    def _():
        meta_state_ref[...] = jnp.zeros_like(meta_state_ref)
        om_ref[...] = _gla_rows(m_ref[...], N_META, meta_state_ref, weights, scratch)

    @pl.when(pl.program_id(1) == 0)
    def _():
        state_ref[...] = meta_state_ref[...]

    ox_ref[0] = _gla_rows(x_ref[0], GLA_CHUNK, state_ref, weights, scratch)


def _gla_block(hx, hm, w_in, w_gu, b_gate, norm_w, w_out, gain, bias):
    bsz, seq, d = hx.shape
    tm = GLA_TILE
    assert seq % tm == 0 and tm % GLA_CHUNK == 0 and N_META <= GLA_CHUNK
    n_chunks = tm // GLA_CHUNK
    x_spec = pl.BlockSpec((1, tm, d), lambda b, t: (b, t, 0))
    state = pltpu.VMEM((GLA_HEADS, GLA_HEAD_K, GLA_HEAD_V), F32)
    return pl.pallas_call(
        _gla_kernel,
        grid=(bsz, seq // tm),
        in_specs=[x_spec, _resident((N_META, d)), _resident(w_in.shape), _resident((GLA_GATE_RANK, GLA_DK)),
                  _resident((1, GLA_DK)), _resident((1, GLA_HEAD_V)), _resident((GLA_DV, d)),
                  _resident((1, d)), _resident((1, d))],
        out_specs=(x_spec, pl.BlockSpec((N_META, d), lambda b, t: (0, 0))),
        out_shape=(jax.ShapeDtypeStruct((bsz, seq, d), F32), jax.ShapeDtypeStruct((N_META, d), F32)),
        scratch_shapes=[
            state, state,
            pltpu.VMEM((tm, GLA_DK), F32),
            pltpu.VMEM((tm, GLA_DK), F32),
            pltpu.VMEM((tm, GLA_DV), BF16),
            pltpu.VMEM((tm, GLA_DK), F32),
            pltpu.VMEM((tm, GLA_DK), F32),
            pltpu.VMEM((tm, GLA_DV), BF16),
            pltpu.VMEM((tm, GLA_DK), BF16),
            pltpu.VMEM((tm, GLA_DK), BF16),
            pltpu.VMEM((tm, GLA_DK), BF16),
            pltpu.VMEM((tm, GLA_DK), BF16),
            pltpu.VMEM((n_chunks, GLA_DK, GLA_CHUNK), BF16),
            pltpu.VMEM((GLA_HEADS, tm, GLA_CHUNK), BF16),
            pltpu.VMEM((n_chunks, GLA_HEADS, GLA_HEAD_K, GLA_HEAD_V), F32),
            pltpu.VMEM((tm, GLA_DV), F32),
        ],
        compiler_params=pltpu.CompilerParams(
            dimension_semantics=("arbitrary", "arbitrary"), vmem_limit_bytes=VMEM_LIMIT_BYTES),
        name="gla_block",
    )(hx, hm, w_in, w_gu, b_gate, norm_w, w_out, gain, bias)


def kernel(x, meta_tokens, ln_gain, ln_bias, ffn_w_in, ffn_w_out, conv_w_in, conv_w, conv_w_out,
           gla_w_in, gla_w_gate_up, gla_b_gate, gla_norm_w, gla_w_out):
    bsz, seq, d = x.shape
    hx = x
    hm = meta_tokens.astype(x.dtype)

    def ffn(hx, hm, i, half):
        ox, om = _ffn_block(hx.reshape(bsz * seq, d), hm, ffn_w_in, ffn_w_out,
                            ln_gain[i, 2 * half][None], ln_bias[i, 2 * half][None], (i, half))
        return ox.reshape(bsz, seq, d), om

    for i in range(DEPTH):
        hx, hm = ffn(hx, hm, i, 0)
        j = i // 2
        gain, bias = ln_gain[i, 1][None], ln_bias[i, 1][None]
        if i % 2 == 0:
            hx, hm = _conv_block(hx, hm, conv_w_in[j].astype(BF16), conv_w[j], conv_w_out[j].astype(BF16),
                                 gain, bias)
        else:
            hx, hm = _gla_block(hx, hm, gla_w_in[j].astype(BF16), gla_w_gate_up[j].astype(BF16),
                                gla_b_gate[j][None], gla_norm_w[j][None], gla_w_out[j].astype(BF16), gain, bias)
        hx, hm = ffn(hx, hm if i + 1 < DEPTH else None, i, 1)
    return hx
```
